```python
import math
import jax, jax.numpy as jnp
from jax import lax
import numpy as np

D_MODEL = 1024
BATCH = 32
SEQ = 2048
DEPTH = 1
DEC_BATCH = 128
DEC_SEQ = 8
PAST_LEN = 16384
PAGE_SIZE = 128

DA_HEADS = 8
DA_KV_HEADS = 4
DA_HEAD_DIM = 64
DA_GROUP = DA_HEADS // DA_KV_HEADS
DA_WIDTH = DA_HEADS * 2 * DA_HEAD_DIM
DA_KV_WIDTH = DA_KV_HEADS * 2 * DA_HEAD_DIM
MLA_HEADS = 8
MLA_Q_RANK = 256
MLA_KV_RANK = 128
MLA_NOPE = 64
MLA_ROPE = 32
MLA_V = 64
MLA_WIDTH = MLA_HEADS * MLA_V
MLA_SCALE = 1.0 / math.sqrt(MLA_NOPE + MLA_ROPE)
ROPE_BASE = 10000.0
REL_BUCKETS = 32
REL_EXACT = REL_BUCKETS // 2
REL_MAX_DIST = 128
D_FF = 2816
MACARON_WEIGHT = 0.5
N_MOD = 9
Q_BLOCK = 128
EPS = 1e-6

kernel_name = "hybrid_diffattn_mla_macaron_adaln_step"


def rmsnorm(x, g):
    xf = x.astype(jnp.float32)
    y = xf * lax.rsqrt(jnp.mean(xf * xf, axis=-1, keepdims=True) + EPS)
    return (y * g.astype(jnp.float32)).astype(x.dtype)


def modulate(x, g, shift, scale):
    return rmsnorm(x, g) * (1.0 + scale[:, None, :]) + shift[:, None, :]


def modulation(c, w, b):
    m = (c @ w + b).reshape(c.shape[0], N_MOD, D_MODEL)
    return [m[:, j] for j in range(N_MOD)]


def swiglu(h, w_up, w_down):
    g, u = jnp.split(h @ w_up, 2, axis=-1)
    return (jax.nn.silu(g) * u) @ w_down


def ffn_sublayer(x, g, shift, scale, gate, w_up, w_down):
    h = modulate(x, g, shift, scale)
    return x + MACARON_WEIGHT * gate[:, None, :] * swiglu(h, w_up, w_down)


def rope_tables(pos, dtype):
    inv = ROPE_BASE ** (-jnp.arange(0, MLA_ROPE, 2, dtype=jnp.float32) / MLA_ROPE)
    ang = pos.astype(jnp.float32)[:, None] * inv[None, :]
    return jnp.cos(ang).astype(dtype), jnp.sin(ang).astype(dtype)


def apply_rope(x, cos, sin):
    x1, x2 = jnp.split(x, 2, axis=-1)
    return jnp.concatenate([x1 * cos - x2 * sin, x1 * sin + x2 * cos], axis=-1)


def t5_bias(q_pos, k_pos, table):
    n = jnp.maximum(q_pos[:, None] - k_pos[None, :], 0)
    nf = jnp.maximum(n, 1).astype(jnp.float32)
    large = REL_EXACT + (jnp.log(nf / REL_EXACT) / math.log(REL_MAX_DIST / REL_EXACT)
                         * (REL_BUCKETS - REL_EXACT)).astype(jnp.int32)
    bucket = jnp.where(n < REL_EXACT, n, jnp.minimum(large, REL_BUCKETS - 1))
    return jnp.moveaxis(table[bucket], -1, 0)


def diff_attention(q, k, v, bias, mask, lam):
    b, t = q.shape[:2]
    qg = q.reshape(b, t, DA_KV_HEADS, DA_GROUP, 2, DA_HEAD_DIM)
    logits = jnp.einsum('btkgmd,blkmd->bkgmtl', qg, k).astype(jnp.float32) * (DA_HEAD_DIM ** -0.5)
    logits = logits + bias.reshape(DA_KV_HEADS, DA_GROUP, 1, t, -1).astype(jnp.float32)
    logits = jnp.where(mask, logits, -jnp.inf)
    p = jax.nn.softmax(logits, axis=-1)
    a = (p[:, :, :, 0] - lam * p[:, :, :, 1]).astype(v.dtype)
    o = jnp.einsum('bkgtl,blkd->btkgd', a, v)
    return o.reshape(b, t, DA_HEADS, 2 * DA_HEAD_DIM)


def mla_attention(q_abs, q_rope, ckv, krope, mask, w_uv):
    logits = (jnp.einsum('bthc,blc->bhtl', q_abs, ckv)
              + jnp.einsum('bthr,blr->bhtl', q_rope, krope)).astype(jnp.float32) * MLA_SCALE
    logits = jnp.where(mask, logits, -jnp.inf)
    p = jax.nn.softmax(logits, axis=-1).astype(ckv.dtype)
    o_lat = jnp.einsum('bhtl,blc->bthc', p, ckv)
    return jnp.einsum('bthc,chd->bthd', o_lat, w_uv)


def mixer_inputs(h, pos, w_in, g_q, w_uq, g_kv, w_uk):
    b, s, _ = h.shape
    sizes = [DA_WIDTH, DA_KV_WIDTH, DA_KV_WIDTH, MLA_Q_RANK, MLA_KV_RANK, MLA_ROPE, D_MODEL, D_MODEL]
    cuts = np.cumsum(sizes)[:-1].tolist()
    q_d, k_d, v_d, q_lat, kv_lat, k_r, g_a, g_b = jnp.split(h @ w_in, cuts, axis=-1)
    q_d = q_d.reshape(b, s, DA_HEADS, 2, DA_HEAD_DIM)
    k_d = k_d.reshape(b, s, DA_KV_HEADS, 2, DA_HEAD_DIM)
    v_d = v_d.reshape(b, s, DA_KV_HEADS, 2 * DA_HEAD_DIM)
    q_m = (rmsnorm(q_lat, g_q) @ w_uq).reshape(b, s, MLA_HEADS, MLA_NOPE + MLA_ROPE)
    q_nope, q_rope = q_m[..., :MLA_NOPE], q_m[..., MLA_NOPE:]
    cos, sin = rope_tables(pos, h.dtype)
    q_rope = apply_rope(q_rope, cos[:, None, :], sin[:, None, :])
    k_r = apply_rope(k_r, cos, sin)
    ckv = rmsnorm(kv_lat, g_kv)
    q_abs = jnp.einsum('bshd,chd->bshc', q_nope, w_uk)
    return q_d, k_d, v_d, q_abs, q_rope, ckv, k_r, g_a, g_b


def merge_branches(o_d, o_m, g_a, g_b, w_pa, w_pb, w_o):
    b, s = o_d.shape[:2]
    y_a = o_d.reshape(b, s, DA_WIDTH) @ w_pa
    y_b = o_m.reshape(b, s, MLA_WIDTH) @ w_pb
    return (jax.nn.sigmoid(g_a) * y_a + jax.nn.sigmoid(g_b) * y_b) @ w_o


def prompt_attention(q_d, k_d, v_d, q_abs, q_rope, ckv, k_r, lam, w_uv, rel_bias):
    b, s = q_d.shape[:2]
    nb = s // Q_BLOCK
    k_pos = jnp.arange(s, dtype=jnp.int32)

    def to_blocks(a):
        return jnp.moveaxis(a.reshape((b, nb, Q_BLOCK) + a.shape[2:]), 1, 0)

    def one_block(args):
        i, qd, qa, qr = args
        q_pos = i * Q_BLOCK + jnp.arange(Q_BLOCK, dtype=jnp.int32)
        mask = k_pos[None, :] <= q_pos[:, None]
        bias = t5_bias(q_pos, k_pos, rel_bias)
        return (diff_attention(qd, k_d, v_d, bias, mask, lam),
                mla_attention(qa, qr, ckv, k_r, mask, w_uv))

    o_d, o_m = lax.map(one_block, (jnp.arange(nb, dtype=jnp.int32), to_blocks(q_d),
                                   to_blocks(q_abs), to_blocks(q_rope)))

    def from_blocks(a):
        return jnp.moveaxis(a, 0, 1).reshape((b, s) + a.shape[3:])

    return from_blocks(o_d), from_blocks(o_m)


def sample_attention(q_d, k_d, v_d, q_abs, q_rope, ckv, k_r, lam, w_uv, bias, mask,
                     cache_k, cache_v, cache_ckv, cache_kr, page_table, layer):
    past_len = page_table.shape[1] * cache_k.shape[2]

    def gather(cache, pages, new):
        past = cache[layer, pages].reshape((past_len,) + cache.shape[3:])
        return jnp.concatenate([past, new.astype(past.dtype)], axis=0)[None]

    def one_seq(args):
        pages, qd, kd, vd, qa, qr, cn, krn = args
        o_d = diff_attention(qd[None], gather(cache_k, pages, kd), gather(cache_v, pages, vd),
                             bias, mask, lam)
        o_m = mla_attention(qa[None], qr[None], gather(cache_ckv, pages, cn),
                            gather(cache_kr, pages, krn), mask, w_uv)
        return o_d[0], o_m[0]

    return lax.map(one_seq, (page_table, q_d, k_d, v_d, q_abs, q_rope, ckv, k_r))


def setup_inputs(seed: int = 0) -> dict:
    key = jax.random.key(seed)
    ks = iter(jax.random.split(key, 48))
    f32 = jnp.float32
    n_pages = PAST_LEN // PAGE_SIZE
    n_pool = (DEC_BATCH * n_pages * 5) // 4
    w_in_cols = DA_WIDTH + 2 * DA_KV_WIDTH + MLA_Q_RANK + MLA_KV_RANK + MLA_ROPE + 2 * D_MODEL

    def nrm(shape, scale=1.0):
        return jax.random.normal(next(ks), shape, f32) * scale

    def gain(shape):
        return 1.0 + 0.02 * jax.random.normal(next(ks), shape, f32)

    page_table = jax.random.permutation(next(ks), n_pool)[: DEC_BATCH * n_pages]
    page_table = page_table.reshape(DEC_BATCH, n_pages).astype(jnp.int32)
    return {
        "x_prompt": nrm((BATCH, SEQ, D_MODEL)),
        "x_sample": nrm((DEC_BATCH, DEC_SEQ, D_MODEL)),
        "c_prompt": nrm((BATCH, D_MODEL)),
        "c_sample": nrm((DEC_BATCH, D_MODEL)),
        "cache_diff_k": nrm((DEPTH, n_pool, PAGE_SIZE, DA_KV_HEADS, 2, DA_HEAD_DIM)),
        "cache_diff_v": nrm((DEPTH, n_pool, PAGE_SIZE, DA_KV_HEADS, 2 * DA_HEAD_DIM)),
        "cache_mla_ckv": nrm((DEPTH, n_pool, PAGE_SIZE, MLA_KV_RANK)),
        "cache_mla_krope": nrm((DEPTH, n_pool, PAGE_SIZE, MLA_ROPE)),
        "page_table": page_table,
        "rel_bias": nrm((REL_BUCKETS, DA_HEADS), 0.2),
        "w_ada": nrm((DEPTH, D_MODEL, N_MOD * D_MODEL), 0.5 * D_MODEL ** -0.5),
        "b_ada": nrm((DEPTH, N_MOD * D_MODEL), 0.01),
        "g_ffn1": gain((DEPTH, D_MODEL)),
        "w_ffn1_up": nrm((DEPTH, D_MODEL, 2 * D_FF), D_MODEL ** -0.5),
        "w_ffn1_down": nrm((DEPTH, D_FF, D_MODEL), D_FF ** -0.5),
        "g_mix": gain((DEPTH, D_MODEL)),
        "w_in": nrm((DEPTH, D_MODEL, w_in_cols), D_MODEL ** -0.5),
        "g_q": gain((DEPTH, MLA_Q_RANK)),
        "w_uq": nrm((DEPTH, MLA_Q_RANK, MLA_HEADS * (MLA_NOPE + MLA_ROPE)), MLA_Q_RANK ** -0.5),
        "g_kv": gain((DEPTH, MLA_KV_RANK)),
        "w_uk": nrm((DEPTH, MLA_KV_RANK, MLA_HEADS, MLA_NOPE), MLA_KV_RANK ** -0.5),
        "w_uv": nrm((DEPTH, MLA_KV_RANK, MLA_HEADS, MLA_V), MLA_KV_RANK ** -0.5),
        "lambda_q1": nrm((DEPTH, DA_HEAD_DIM), 0.1),
        "lambda_k1": nrm((DEPTH, DA_HEAD_DIM), 0.1),
        "lambda_q2": nrm((DEPTH, DA_HEAD_DIM), 0.1),
        "lambda_k2": nrm((DEPTH, DA_HEAD_DIM), 0.1),
        "g_subln": gain((DEPTH, 2 * DA_HEAD_DIM)),
        "w_pa": nrm((DEPTH, DA_WIDTH, D_MODEL), DA_WIDTH ** -0.5),
        "w_pb": nrm((DEPTH, MLA_WIDTH, D_MODEL), MLA_WIDTH ** -0.5),
        "w_o": nrm((DEPTH, D_MODEL, D_MODEL), D_MODEL ** -0.5),
        "g_ffn2": gain((DEPTH, D_MODEL)),
        "w_ffn2_up": nrm((DEPTH, D_MODEL, 2 * D_FF), D_MODEL ** -0.5),
        "w_ffn2_down": nrm((DEPTH, D_FF, D_MODEL), D_FF ** -0.5),
        "g_final": gain((D_MODEL,)),
    }


def reference(x_prompt, x_sample, c_prompt, c_sample, cache_diff_k, cache_diff_v, cache_mla_ckv,
              cache_mla_krope, page_table, rel_bias, w_ada, b_ada, g_ffn1, w_ffn1_up, w_ffn1_down,
              g_mix, w_in, g_q, w_uq, g_kv, w_uk, w_uv, lambda_q1, lambda_k1, lambda_q2, lambda_k2,
              g_subln, w_pa, w_pb, w_o, g_ffn2, w_ffn2_up, w_ffn2_down, g_final):
    f32 = jnp.float32
    seq = x_prompt.shape[1]
    dec_seq = x_sample.shape[1]
    past_len = page_table.shape[1] * cache_diff_k.shape[2]
    pos_p = jnp.arange(seq, dtype=jnp.int32)
    pos_s = past_len + jnp.arange(dec_seq, dtype=jnp.int32)
    k_pos_s = jnp.arange(past_len + dec_seq, dtype=jnp.int32)
    mask_s = k_pos_s[None, :] <= pos_s[:, None]
    bias_s = t5_bias(pos_s, k_pos_s, rel_bias)

    xp, xs = x_prompt, x_sample
    st_p, st_s = [], []
    for l in range(DEPTH):
        lam_init = 0.8 - 0.6 * math.exp(-0.3 * l)
        lam = (jnp.exp(jnp.sum(lambda_q1[l].astype(f32) * lambda_k1[l].astype(f32)))
               - jnp.exp(jnp.sum(lambda_q2[l].astype(f32) * lambda_k2[l].astype(f32))) + lam_init)

        def layer(x, c, pos, attend):
            m = modulation(c, w_ada[l], b_ada[l])
            x = ffn_sublayer(x, g_ffn1[l], m[0], m[1], m[2], w_ffn1_up[l], w_ffn1_down[l])
            h = modulate(x, g_mix[l], m[3], m[4])
            q_d, k_d, v_d, q_abs, q_rope, ckv, k_r, g_a, g_b = mixer_inputs(
                h, pos, w_in[l], g_q[l], w_uq[l], g_kv[l], w_uk[l])
            o_d, o_m = attend(q_d, k_d, v_d, q_abs, q_rope, ckv, k_r)
            o_d = rmsnorm(o_d, g_subln[l]) * (1.0 - lam_init)
            x = x + m[5][:, None, :] * merge_branches(o_d, o_m, g_a, g_b, w_pa[l], w_pb[l], w_o[l])
            x = ffn_sublayer(x, g_ffn2[l], m[6], m[7], m[8], w_ffn2_up[l], w_ffn2_down[l])
            return x, (k_d, v_d, ckv, k_r)

        attend_p = lambda qd, kd, vd, qa, qr, cn, kr: prompt_attention(
            qd, kd, vd, qa, qr, cn, kr, lam, w_uv[l], rel_bias)
        attend_s = lambda qd, kd, vd, qa, qr, cn, kr: sample_attention(
            qd, kd, vd, qa, qr, cn, kr, lam, w_uv[l], bias_s, mask_s,
            cache_diff_k, cache_diff_v, cache_mla_ckv, cache_mla_krope, page_table, l)
        xp, new_p = layer(xp, c_prompt, pos_p, attend_p)
        xs, new_s = layer(xs, c_sample, pos_s, attend_s)
        st_p.append(new_p)
        st_s.append(new_s)

    y_prompt = rmsnorm(xp, g_final)
    y_sample = rmsnorm(xs, g_final)
    return (y_prompt, y_sample,
            jnp.stack([s[0] for s in st_p]), jnp.stack([s[1] for s in st_p]),
            jnp.stack([s[2] for s in st_p]), jnp.stack([s[3] for s in st_p]),
            jnp.stack([s[0] for s in st_s]), jnp.stack([s[1] for s in st_s]),
            jnp.stack([s[2] for s in st_s]), jnp.stack([s[3] for s in st_s]))
```

```python
import functools
import math

import jax
import jax.numpy as jnp
from jax import lax
from jax.experimental import pallas as pl
from jax.experimental.pallas import tpu as pltpu

F32 = jnp.float32
BF16 = jnp.bfloat16

DA_HEADS = 8
DA_KV_HEADS = 4
DA_HEAD_DIM = 64
DA_GROUP = DA_HEADS // DA_KV_HEADS
DA_V = 2 * DA_HEAD_DIM
MLA_HEADS = 8
MLA_Q_RANK = 256
MLA_KV_RANK = 128
MLA_NOPE = 64
MLA_ROPE = 32
MLA_V = 64
MLA_SCALE = 1.0 / math.sqrt(MLA_NOPE + MLA_ROPE)
ROPE_BASE = 10000.0
REL_BUCKETS = 32
REL_EXACT = REL_BUCKETS // 2
REL_MAX_DIST = 128
MACARON_WEIGHT = 0.5
N_MOD = 9
EPS = 1e-6
NEG = -1e30

LANES = 128
SUBLANES = 8
VMEM_LIMIT_BYTES = 58 * 1024 * 1024

_BUCKET_STARTS = tuple(
    int(math.ceil(REL_EXACT * (REL_MAX_DIST / REL_EXACT) ** (k / (REL_BUCKETS - REL_EXACT))))
    for k in range(1, REL_BUCKETS - REL_EXACT))

Q_TILE = 256
PAGES_PER_STEP = 8


def _rms(x, g):
    return x * lax.rsqrt(jnp.mean(x * x, axis=-1, keepdims=True) + EPS) * g


def _dot(a, b):
    return jnp.dot(a, b, preferred_element_type=F32)


def _dot_nt(a, b):
    return lax.dot_general(a, b, (((1,), (1,)), ((), ())), preferred_element_type=F32)


def _params(*semantics):
    return pltpu.CompilerParams(dimension_semantics=semantics, vmem_limit_bytes=VMEM_LIMIT_BYTES)


def _resident(shape):
    zeros = (0,) * len(shape)
    return pl.BlockSpec(shape, lambda *_: zeros, pipeline_mode=pl.Buffered(1))


def _mod_kernel(c_ref, w_ref, b_ref, o_ref):
    o_ref[...] = _dot(c_ref[...].astype(BF16), w_ref[...].astype(BF16)) + b_ref[...]


def _modulation(c, w, b):
    nb, d = c.shape
    n = w.shape[1]
    out = pl.pallas_call(
        _mod_kernel,
        grid=(n // d,),
        in_specs=[pl.BlockSpec((nb, d), lambda j: (0, 0)),
                  pl.BlockSpec((d, d), lambda j: (0, j)),
                  pl.BlockSpec((1, d), lambda j: (0, j))],
        out_specs=pl.BlockSpec((nb, d), lambda j: (0, j)),
        out_shape=jax.ShapeDtypeStruct((nb, n), F32),
        compiler_params=_params("parallel"),
        name="modulation",
    )(c, w, b.reshape(1, n))
    return out.reshape(nb, N_MOD, 1, d)


def _prep_kernel(wqn_ref, wukt_ref, wuv_ref, wpb_ref, lq1_ref, lk1_ref, lq2_ref, lk2_ref,
                 wqabs_ref, wuvpb_ref, lam_ref, *, lam_init):
    hi = lax.Precision.HIGHEST
    for h in range(MLA_HEADS):
        qa = jnp.dot(wqn_ref[h], wukt_ref[h], precision=hi, preferred_element_type=F32)
        wqabs_ref[:, h * MLA_KV_RANK:(h + 1) * MLA_KV_RANK] = (qa * MLA_SCALE).astype(BF16)
        vp = jnp.dot(wuv_ref[h], wpb_ref[h], precision=hi, preferred_element_type=F32)
        wuvpb_ref[h * MLA_KV_RANK:(h + 1) * MLA_KV_RANK, :] = vp.astype(BF16)
    s1 = jnp.sum(lq1_ref[...] * lk1_ref[...], axis=-1, keepdims=True)
    s2 = jnp.sum(lq2_ref[...] * lk2_ref[...], axis=-1, keepdims=True)
    lam_ref[...] = jnp.exp(s1) - jnp.exp(s2) + lam_init


def _prep(w_uq, w_uk, w_uv, w_pb, lq1, lk1, lq2, lk2, lam_init):
    d_model = w_pb.shape[1]
    wq = w_uq.reshape(MLA_Q_RANK, MLA_HEADS, MLA_NOPE + MLA_ROPE)
    wqn = jnp.transpose(wq[:, :, :MLA_NOPE], (1, 0, 2))
    wukt = jnp.transpose(w_uk, (1, 2, 0))
    wuv = jnp.transpose(w_uv, (1, 0, 2))
    wpb = w_pb.reshape(MLA_HEADS, MLA_V, d_model)
    row = lambda a: a.reshape(1, -1)
    return pl.pallas_call(
        functools.partial(_prep_kernel, lam_init=lam_init),
        out_shape=(jax.ShapeDtypeStruct((MLA_Q_RANK, MLA_HEADS * MLA_KV_RANK), BF16),
                   jax.ShapeDtypeStruct((MLA_HEADS * MLA_KV_RANK, d_model), BF16),
                   jax.ShapeDtypeStruct((1, 1), F32)),
        compiler_params=pltpu.CompilerParams(vmem_limit_bytes=VMEM_LIMIT_BYTES),
        name="fold_weights",
    )(wqn, wukt, wuv, wpb, row(lq1), row(lk1), row(lq2), row(lk2))


def _bucket(n):
    bucket = jnp.full(n.shape, REL_EXACT, jnp.int32)
    for start in _BUCKET_STARTS:
        bucket = bucket + (n >= start).astype(jnp.int32)
    return jnp.where(n < REL_EXACT, n, bucket)


def _bias_rel(bucket, tab_ref, h):
    far = tab_ref[REL_BUCKETS - 1, h]
    val = jnp.zeros(bucket.shape, F32)
    for b in range(REL_BUCKETS - 1):
        val = jnp.where(bucket == b, tab_ref[b, h] - far, val)
    return val


def _bias_kernel(tab_ref, tiles_ref, last_ref, new_ref, *, tq, page, dec_seq, new_pad):
    r = lax.broadcasted_iota(jnp.int32, (tq, tq), 0)
    c = lax.broadcasted_iota(jnp.int32, (tq, tq), 1)
    for t in range(2):
        n = r - c + t * tq
        bk = _bucket(jnp.maximum(n, 0))
        for h in range(DA_HEADS):
            tiles_ref[t, h] = jnp.where(n >= 0, _bias_rel(bk, tab_ref, h), NEG)
        tiles_ref[t, DA_HEADS] = jnp.where(n >= 0, 0.0, NEG)
    tr = lax.broadcasted_iota(jnp.int32, (dec_seq, page), 0)
    tc = lax.broadcasted_iota(jnp.int32, (dec_seq, page), 1)
    bk = _bucket(tr + page - tc)
    for h in range(DA_HEADS):
        last_ref[h] = _bias_rel(bk, tab_ref, h)
    ur = lax.broadcasted_iota(jnp.int32, (dec_seq, new_pad), 0)
    uc = lax.broadcasted_iota(jnp.int32, (dec_seq, new_pad), 1)
    n = ur - uc
    ok = (n >= 0) & (uc < dec_seq)
    bk = _bucket(jnp.maximum(n, 0))
    for h in range(DA_HEADS):
        new_ref[h] = jnp.where(ok, _bias_rel(bk, tab_ref, h), NEG)
    new_ref[DA_HEADS] = jnp.where(ok, 0.0, NEG)


def _bias_tiles(rel_bias, tq, page, dec_seq, new_pad):
    return pl.pallas_call(
        functools.partial(_bias_kernel, tq=tq, page=page, dec_seq=dec_seq, new_pad=new_pad),
        in_specs=[pl.BlockSpec(memory_space=pltpu.SMEM)],
        out_shape=(jax.ShapeDtypeStruct((2, DA_HEADS + 1, tq, tq), F32),
                   jax.ShapeDtypeStruct((DA_HEADS, dec_seq, page), F32),
                   jax.ShapeDtypeStruct((DA_HEADS + 1, dec_seq, new_pad), F32)),
        compiler_params=pltpu.CompilerParams(vmem_limit_bytes=VMEM_LIMIT_BYTES),
        name="bias_tiles",
    )(rel_bias)


def _token_tiling(b, s):
    if s >= 512 and s % 512 == 0:
        return 1, 512
    if s >= 256 and s % 256 == 0:
        return 1, 256
    if s >= LANES:
        assert s % LANES == 0
        return 1, LANES
    assert s % SUBLANES == 0
    bb = min(b, 256 // s)
    assert b % bb == 0
    return bb, s


def _x_spec(bb, ts, d):
    return pl.BlockSpec((bb, ts, d), lambda i, j: (i, j, 0))


def _mod_spec(bb, d, group):
    return pl.BlockSpec((bb, 3, 1, d), lambda i, j: (i, group, 0, 0))


def _rows_spec(rows, width, ns):
    return pl.BlockSpec((rows, width), lambda i, j: (i * ns + j, 0))


def _modulated(x_ref, mod_ref, g_ref):
    x = x_ref[...]
    h = _rms(x, g_ref[...]) * (1.0 + mod_ref[:, 1]) + mod_ref[:, 0]
    bb, ts, d = x.shape
    return x, h.reshape(bb * ts, d).astype(BF16)


def _ffn_kernel(x_ref, mod_ref, g_ref, wup_ref, wdn_ref, *rest, n_chunks, final_norm):
    if final_norm:
        gf_ref, o_ref = rest
    else:
        (o_ref,) = rest
    x, hb = _modulated(x_ref, mod_ref, g_ref)
    d_ff = wdn_ref.shape[0]
    cf = d_ff // n_chunks
    acc = None
    for c in range(n_chunks):
        g = _dot(hb, wup_ref[:, c * cf:(c + 1) * cf])
        u = _dot(hb, wup_ref[:, d_ff + c * cf:d_ff + (c + 1) * cf])
        a = (g * jax.nn.sigmoid(g) * u).astype(BF16)
        part = _dot(a, wdn_ref[c * cf:(c + 1) * cf, :])
        acc = part if acc is None else acc + part
    out = x + (MACARON_WEIGHT * mod_ref[:, 2]) * acc.reshape(x.shape)
    if final_norm:
        out = _rms(out, gf_ref[...])
    o_ref[...] = out


def _ffn(x, mod, group, g, w_up, w_down, g_final=None):
    b, s, d = x.shape
    bb, ts = _token_tiling(b, s)
    d_ff = w_down.shape[0]
    n_chunks = 2 if d_ff % (2 * LANES) == 0 else 1
    in_specs = [_x_spec(bb, ts, d), _mod_spec(bb, d, group), _resident((1, d)),
                _resident(w_up.shape), _resident(w_down.shape)]
    args = [x, mod, g.reshape(1, d), w_up, w_down]
    if g_final is not None:
        in_specs.append(_resident((1, d)))
        args.append(g_final.reshape(1, d))
    return pl.pallas_call(
        functools.partial(_ffn_kernel, n_chunks=n_chunks, final_norm=g_final is not None),
        grid=(b // bb, s // ts),
        in_specs=in_specs,
        out_specs=_x_spec(bb, ts, d),
        out_shape=jax.ShapeDtypeStruct(x.shape, F32),
        compiler_params=_params("parallel", "parallel"),
        name="ffn_final" if g_final is not None else "ffn",
    )(*args)


def _rope(z, cos, sin_lo, sin_hi):
    half = MLA_ROPE // 2
    return z * cos + pltpu.roll(z, LANES - half, 1) * sin_lo + pltpu.roll(z, half, 1) * sin_hi


def _mixer_in_kernel(x_ref, mod_ref, g_ref, w_ref, gq_ref, gkv_ref, wqabs_ref, wqrope_ref,
                     cos_ref, slo_ref, shi_ref,
                     qd_ref, kd_ref, kdb_ref, vd_ref, vaug_ref, qcat_ref, kvcat_ref, ckv_ref, kr_ref):
    bb, ts, d = x_ref.shape
    rows = bb * ts
    _, hb = _modulated(x_ref, mod_ref, g_ref)
    qw = DA_HEADS * DA_V
    kw = DA_KV_HEADS * DA_V
    qd_ref[...] = (_dot(hb, w_ref[:, :qw]) * (DA_HEAD_DIM ** -0.5)).astype(BF16)
    kd = _dot(hb, w_ref[:, qw:qw + kw])
    kd_ref[...] = kd
    kdb_ref[...] = kd.astype(BF16)
    vd = _dot(hb, w_ref[:, qw + kw:qw + 2 * kw])
    vd_ref[...] = vd
    ones = jnp.ones((rows, DA_V), BF16)
    for k in range(DA_KV_HEADS):
        vaug_ref[:, 2 * k * DA_V:(2 * k + 1) * DA_V] = vd[:, k * DA_V:(k + 1) * DA_V].astype(BF16)
        vaug_ref[:, (2 * k + 1) * DA_V:(2 * k + 2) * DA_V] = ones
    base = qw + 2 * kw
    lat = _dot(hb, w_ref[:, base:base + MLA_Q_RANK + MLA_KV_RANK + LANES])
    qn = _rms(lat[:, :MLA_Q_RANK], gq_ref[...]).astype(BF16)
    ckv = _rms(lat[:, MLA_Q_RANK:MLA_Q_RANK + MLA_KV_RANK], gkv_ref[...])
    kr = lat[:, MLA_Q_RANK + MLA_KV_RANK:]

    def table(ref):
        return jnp.broadcast_to(ref[...], (bb, ts, LANES)).reshape(rows, LANES)

    cos, slo, shi = table(cos_ref), table(slo_ref), table(shi_ref)
    qabs = _dot(qn, wqabs_ref[...])
    zq = _dot(qn, wqrope_ref[...])
    for h in range(MLA_HEADS):
        lo = 2 * h * LANES
        qcat_ref[:, lo:lo + LANES] = qabs[:, h * LANES:(h + 1) * LANES].astype(BF16)
        qcat_ref[:, lo + LANES:lo + 2 * LANES] = _rope(zq[:, h * LANES:(h + 1) * LANES],
                                                       cos, slo, shi).astype(BF16)
    krr = _rope(kr, cos, slo, shi)
    ckv_ref[...] = ckv
    kr_ref[...] = krr[:, :MLA_ROPE]
    lane = lax.broadcasted_iota(jnp.int32, (rows, LANES), 1)
    kvcat_ref[:, :LANES] = ckv.astype(BF16)
    kvcat_ref[:, LANES:] = jnp.where(lane == LANES - 1, 1.0, krr).astype(BF16)


def _mixer_in(x, mod, g, w_qkv, g_q, g_kv, w_qabs, w_qrope, tables):
    b, s, d = x.shape
    bb, ts = _token_tiling(b, s)
    ns = s // ts
    rows = bb * ts
    n = b * s
    widths = [(DA_HEADS * DA_V, BF16), (DA_KV_HEADS * DA_V, F32), (DA_KV_HEADS * DA_V, BF16),
              (DA_KV_HEADS * DA_V, F32), (2 * DA_KV_HEADS * DA_V, BF16),
              (2 * MLA_HEADS * LANES, BF16), (2 * LANES, BF16), (MLA_KV_RANK, F32), (MLA_ROPE, F32)]
    tab_spec = pl.BlockSpec((1, ts, LANES), lambda i, j: (0, j, 0))
    return pl.pallas_call(
        _mixer_in_kernel,
        grid=(b // bb, ns),
        in_specs=[_x_spec(bb, ts, d), _mod_spec(bb, d, 1), _resident((1, d)), _resident(w_qkv.shape),
                  _resident((1, MLA_Q_RANK)), _resident((1, MLA_KV_RANK)),
                  _resident(w_qabs.shape), _resident(w_qrope.shape), tab_spec, tab_spec, tab_spec],
        out_specs=[_rows_spec(rows, w, ns) for w, _ in widths],
        out_shape=[jax.ShapeDtypeStruct((n, w), dt) for w, dt in widths],
        compiler_params=_params("parallel", "parallel"),
        name="mixer_in",
    )(x, mod, g.reshape(1, d), w_qkv, g_q.reshape(1, -1), g_kv.reshape(1, -1), w_qabs, w_qrope, *tables)


def _merge_kernel(x_ref, mod_ref, g_ref, wg_ref, od_ref, olat_ref, wpa_ref, wuvpb_ref, wo_ref, o_ref):
    x, hb = _modulated(x_ref, mod_ref, g_ref)
    d = x.shape[-1]
    ya = _dot(od_ref[...].astype(BF16), wpa_ref[...])
    yb = _dot(olat_ref[...].astype(BF16), wuvpb_ref[...])
    ga = jax.nn.sigmoid(_dot(hb, wg_ref[:, :d]))
    gb = jax.nn.sigmoid(_dot(hb, wg_ref[:, d:]))
    mix = (ga * ya + gb * yb).astype(BF16)
    o_ref[...] = x + mod_ref[:, 2] * _dot(mix, wo_ref[...]).reshape(x.shape)


def _merge(x, mod, g, w_gates, od, olat, w_pa, w_uvpb, w_o):
    b, s, d = x.shape
    bb, ts = _token_tiling(b, s)
    ns = s // ts
    rows = bb * ts
    return pl.pallas_call(
        _merge_kernel,
        grid=(b // bb, ns),
        in_specs=[_x_spec(bb, ts, d), _mod_spec(bb, d, 1), _resident((1, d)), _resident(w_gates.shape),
                  _rows_spec(rows, od.shape[1], ns), _rows_spec(rows, olat.shape[1], ns),
                  _resident(w_pa.shape), _resident(w_uvpb.shape), _resident(w_o.shape)],
        out_specs=_x_spec(bb, ts, d),
        out_shape=jax.ShapeDtypeStruct(x.shape, F32),
        compiler_params=_params("parallel", "parallel"),
        name="merge",
    )(x, mod, g.reshape(1, d), w_gates, od, olat, w_pa, w_uvpb, w_o)


def _online_update(s, m, acc, v):
    m_new = jnp.maximum(m, jnp.max(s, axis=-1, keepdims=True))
    p = jnp.exp(s - m_new)
    return m_new, jnp.exp(m - m_new) * acc + _dot(p.astype(BF16), v), p


def _diff_out(o0, l0, o1, l1, lam, g_subln, out_scale):
    o = o0 / l0 - lam * (o1 / l1)
    return _rms(o, g_subln) * out_scale


def _half_masks(rows):
    lane = lax.broadcasted_iota(jnp.int32, (rows, DA_V), 1)
    return lane < DA_HEAD_DIM


def _prompt_attn_kernel(lam_ref, qd_ref, qcat_ref, kdb_ref, vaug_ref, kvcat_ref, tiles_ref, gs_ref,
                        od_ref, olat_ref, *, tq, out_scale):
    i = pl.program_id(1)
    lam = lam_ref[0, 0]
    near_lo = jnp.maximum(i - 1, 0)
    first = _half_masks(tq)

    def kv_rows(j):
        return pl.ds(pl.multiple_of(j * tq, tq), tq)

    for k in range(DA_KV_HEADS):
        parts = []
        for m in range(2):
            for g in range(DA_GROUP):
                hq = k * DA_GROUP + g
                q = qd_ref[0, :, hq * DA_V:(hq + 1) * DA_V]
                keep = first if m == 0 else jnp.logical_not(first)
                parts.append(jnp.where(keep, q, jnp.zeros_like(q)))
        lhs = jnp.concatenate(parts, axis=0)

        def scores(j, lhs=lhs, k=k):
            return _dot_nt(lhs, kdb_ref[0, kv_rows(j), k * DA_V:(k + 1) * DA_V])

        def values(j, k=k):
            return vaug_ref[0, kv_rows(j), 2 * k * DA_V:(2 * k + 2) * DA_V]

        def far(j, carry):
            m_run, acc = carry
            m_run, acc, _ = _online_update(scores(j), m_run, acc, values(j))
            return m_run, acc

        def near(j, carry, k=k):
            m_run, acc = carry
            t = i - j
            b0 = tiles_ref[t, k * DA_GROUP]
            b1 = tiles_ref[t, k * DA_GROUP + 1]
            s = scores(j) + jnp.concatenate([b0, b1, b0, b1], axis=0)
            m_run, acc, _ = _online_update(s, m_run, acc, values(j))
            return m_run, acc

        init = (jnp.full((4 * tq, 1), -jnp.inf, F32), jnp.zeros((4 * tq, 2 * DA_V), F32))
        carry = lax.fori_loop(0, near_lo, far, init)
        _, acc = lax.fori_loop(near_lo, i + 1, near, carry)
        for g in range(DA_GROUP):
            a0 = acc[g * tq:(g + 1) * tq]
            a1 = acc[(DA_GROUP + g) * tq:(DA_GROUP + g + 1) * tq]
            o = _diff_out(a0[:, :DA_V], a0[:, DA_V:DA_V + 1], a1[:, :DA_V], a1[:, DA_V:DA_V + 1],
                          lam, gs_ref[...], out_scale)
            hq = k * DA_GROUP + g
            od_ref[0, :, hq * DA_V:(hq + 1) * DA_V] = o.astype(BF16)

    for h in range(MLA_HEADS):
        q = qcat_ref[0, :, 2 * h * LANES:(2 * h + 2) * LANES]

        def far_m(j, carry, q=q):
            m_run, acc = carry
            kv = kvcat_ref[0, kv_rows(j), :]
            m_run, acc, _ = _online_update(_dot_nt(q, kv), m_run, acc, kv)
            return m_run, acc

        def near_m(j, carry, q=q):
            m_run, acc = carry
            kv = kvcat_ref[0, kv_rows(j), :]
            s = _dot_nt(q, kv) + tiles_ref[i - j, DA_HEADS]
            m_run, acc, _ = _online_update(s, m_run, acc, kv)
            return m_run, acc

        init = (jnp.full((tq, 1), -jnp.inf, F32), jnp.zeros((tq, 2 * LANES), F32))
        carry = lax.fori_loop(0, near_lo, far_m, init)
        _, acc = lax.fori_loop(near_lo, i + 1, near_m, carry)
        olat_ref[0, :, h * LANES:(h + 1) * LANES] = (
            acc[:, :MLA_KV_RANK] / acc[:, 2 * LANES - 1:]).astype(BF16)


def _prompt_attention(lam, qd, qcat, kdb, vaug, kvcat, tiles, g_subln, b, s, out_scale):
    tq = tiles.shape[-1]
    assert s % tq == 0
    v3 = lambda a: a.reshape(b, s, a.shape[-1])
    qspec = lambda w: pl.BlockSpec((1, tq, w), lambda bi, i: (bi, i, 0))
    kvspec = lambda w: pl.BlockSpec((1, s, w), lambda bi, i: (bi, 0, 0))
    od, olat = pl.pallas_call(
        functools.partial(_prompt_attn_kernel, tq=tq, out_scale=out_scale),
        grid=(b, s // tq),
        in_specs=[pl.BlockSpec(memory_space=pltpu.SMEM),
                  qspec(qd.shape[-1]), qspec(qcat.shape[-1]),
                  kvspec(kdb.shape[-1]), kvspec(vaug.shape[-1]), kvspec(kvcat.shape[-1]),
                  _resident(tiles.shape), _resident((1, DA_V))],
        out_specs=[qspec(DA_HEADS * DA_V), qspec(MLA_HEADS * MLA_KV_RANK)],
        out_shape=[jax.ShapeDtypeStruct((b, s, DA_HEADS * DA_V), BF16),
                   jax.ShapeDtypeStruct((b, s, MLA_HEADS * MLA_KV_RANK), BF16)],
        compiler_params=_params("parallel", "arbitrary"),
        name="prompt_attention",
    )(lam, v3(qd), v3(qcat), v3(kdb), v3(vaug), v3(kvcat), tiles, g_subln.reshape(1, DA_V))
    return od.reshape(b * s, -1), olat.reshape(b * s, -1)


def _sample_attn_kernel(pt_ref, lam_ref, qd_ref, qcat_ref, kn_ref, vn_ref, kvn_ref,
                        blast_ref, bnew_ref, mnew_ref, gs_ref, *rest, pages, page, dec_seq, out_scale):
    del pt_ref
    k_refs = rest[:pages]
    v_refs = rest[pages:2 * pages]
    c_refs = rest[2 * pages:3 * pages]
    r_refs = rest[3 * pages:4 * pages]
    od_ref, olat_ref = rest[4 * pages:4 * pages + 2]
    (qbd_scr, qm_scr, kb_scr, vb_scr, cb_scr, rb_scr,
     md_scr, ld_scr, ad_scr, mm_scr, lm_scr, am_scr) = rest[4 * pages + 2:]
    g = pl.program_id(1)
    ng = pl.num_programs(1)
    t = dec_seq

    @pl.when(g == 0)
    def _():
        qd = qd_ref[0]
        first = _half_masks(t)
        zero = jnp.zeros((t, DA_V), F32)
        blocks = []
        for k in range(DA_KV_HEADS):
            for m in range(2):
                for gg in range(DA_GROUP):
                    hq = k * DA_GROUP + gg
                    keep = first if m == 0 else jnp.logical_not(first)
                    piece = jnp.where(keep, qd[:, hq * DA_V:(hq + 1) * DA_V], 0.0)
                    blocks.append(jnp.concatenate(
                        [piece if kk == k else zero for kk in range(DA_KV_HEADS)], axis=1))
        qbd_scr[...] = jnp.concatenate(blocks, axis=0).astype(BF16)
        qc = qcat_ref[0]
        qm_scr[...] = jnp.concatenate(
            [qc[:, 2 * h * LANES:(2 * h + 2) * LANES] for h in range(MLA_HEADS)], axis=0).astype(BF16)
        md_scr[...] = jnp.full(md_scr.shape, -jnp.inf, F32)
        mm_scr[...] = jnp.full(mm_scr.shape, -jnp.inf, F32)
        ld_scr[...] = jnp.zeros(ld_scr.shape, F32)
        lm_scr[...] = jnp.zeros(lm_scr.shape, F32)
        ad_scr[...] = jnp.zeros(ad_scr.shape, F32)
        am_scr[...] = jnp.zeros(am_scr.shape, F32)

    for p in range(pages):
        keys = slice(p * page, (p + 1) * page)
        kb_scr[:, keys] = k_refs[p][0, 0].astype(BF16)
        for k in range(DA_KV_HEADS):
            vb_scr[k, keys, :] = v_refs[p][0, 0, pl.ds(k, page, stride=DA_KV_HEADS), :].astype(BF16)
        cb_scr[keys, :] = c_refs[p][0, 0].astype(BF16)
        rb_scr[:, keys] = r_refs[p][0, 0].astype(BF16)

    def update_diff(s, vals):
        m_old = md_scr[...]
        m_new = jnp.maximum(m_old, jnp.max(s, axis=-1, keepdims=True))
        pr = jnp.exp(s - m_new)
        alpha = jnp.exp(m_old - m_new)
        md_scr[...] = m_new
        ld_scr[...] = alpha * ld_scr[...] + jnp.sum(pr, axis=-1, keepdims=True)
        pb = pr.astype(BF16)
        rk = 2 * DA_GROUP * t
        for k in range(DA_KV_HEADS):
            pv = _dot(pb[k * rk:(k + 1) * rk], vals(k))
            ad_scr[k * rk:(k + 1) * rk, :] = alpha[k * rk:(k + 1) * rk] * ad_scr[k * rk:(k + 1) * rk, :] + pv

    def update_mla(s, vals):
        m_old = mm_scr[...]
        m_new = jnp.maximum(m_old, jnp.max(s, axis=-1, keepdims=True))
        pr = jnp.exp(s - m_new)
        alpha = jnp.exp(m_old - m_new)
        mm_scr[...] = m_new
        lm_scr[...] = alpha * lm_scr[...] + jnp.sum(pr, axis=-1, keepdims=True)
        am_scr[...] = alpha * am_scr[...] + _dot(pr.astype(BF16), vals)

    is_last = (g == ng - 1).astype(F32)
    s_d = _dot(qbd_scr[...], kb_scr[...]) + is_last * blast_ref[...]
    update_diff(s_d, lambda k: vb_scr[k])
    s_m = (_dot_nt(qm_scr[:, :MLA_KV_RANK], cb_scr[...])
           + _dot(qm_scr[:, MLA_KV_RANK:MLA_KV_RANK + MLA_ROPE], rb_scr[...]))
    update_mla(s_m, cb_scr[...])

    @pl.when(g == ng - 1)
    def _():
        pad = bnew_ref.shape[-1] - t
        zpad = lambda a: jnp.concatenate([a, jnp.zeros((pad, a.shape[-1]), a.dtype)], axis=0)
        kn = zpad(kn_ref[0]).astype(BF16)
        vn = zpad(vn_ref[0]).astype(BF16)
        kvn = zpad(kvn_ref[0]).astype(BF16)
        update_diff(_dot_nt(qbd_scr[...], kn) + bnew_ref[...], lambda k: vn[:, k * DA_V:(k + 1) * DA_V])
        update_mla(_dot_nt(qm_scr[...], kvn) + mnew_ref[...], kvn[:, :MLA_KV_RANK])
        lam = lam_ref[0, 0]
        for k in range(DA_KV_HEADS):
            for gg in range(DA_GROUP):
                r0 = ((k * 2 + 0) * DA_GROUP + gg) * t
                r1 = ((k * 2 + 1) * DA_GROUP + gg) * t
                o = _diff_out(ad_scr[r0:r0 + t, :], ld_scr[r0:r0 + t, :],
                              ad_scr[r1:r1 + t, :], ld_scr[r1:r1 + t, :], lam, gs_ref[...], out_scale)
                hq = k * DA_GROUP + gg
                od_ref[0, :, hq * DA_V:(hq + 1) * DA_V] = o
        for h in range(MLA_HEADS):
            olat_ref[0, :, h * LANES:(h + 1) * LANES] = (
                am_scr[h * t:(h + 1) * t, :] / lm_scr[h * t:(h + 1) * t, :])


def _sample_attention(page_table, lam, qd, qcat, kd, vd, kvcat, blast, bnew, mnew, g_subln,
                      cache_k, cache_v, cache_ckv, cache_kr, layer, out_scale):
    nseq, n_pages = page_table.shape
    page = cache_k.shape[2]
    t = qd.shape[0] // nseq
    pages = min(PAGES_PER_STEP, n_pages)
    assert n_pages % pages == 0
    n_keys = pages * page
    kvw = DA_KV_HEADS * DA_V
    depth, n_pool = cache_k.shape[:2]
    ck = jnp.transpose(cache_k, (0, 1, 3, 4, 5, 2)).reshape(depth, n_pool, kvw, page)
    cv = cache_v.reshape(depth, n_pool, page * DA_KV_HEADS, DA_V)
    cr = jnp.transpose(cache_kr, (0, 1, 3, 2))
    seq3 = lambda a: a.reshape(nseq, t, a.shape[-1]).astype(F32)
    seq_spec = lambda w: pl.BlockSpec((1, t, w), lambda b, g, pt: (b, 0, 0))
    const = lambda shape: pl.BlockSpec(shape, lambda b, g, pt: (0,) * len(shape))

    def page_spec(rows, width, p):
        return pl.BlockSpec((1, 1, rows, width), lambda b, g, pt: (layer, pt[b, g * pages + p], 0, 0))

    rows_d = DA_KV_HEADS * 2 * DA_GROUP * t
    rows_m = MLA_HEADS * t
    in_specs = ([pl.BlockSpec(memory_space=pltpu.SMEM),
                 seq_spec(DA_HEADS * DA_V), seq_spec(2 * MLA_HEADS * LANES), seq_spec(kvw), seq_spec(kvw),
                 seq_spec(2 * LANES), const(blast.shape), const(bnew.shape), const(mnew.shape),
                 const((1, DA_V))]
                + [page_spec(kvw, page, p) for p in range(pages)]
                + [page_spec(page * DA_KV_HEADS, DA_V, p) for p in range(pages)]
                + [page_spec(page, MLA_KV_RANK, p) for p in range(pages)]
                + [page_spec(MLA_ROPE, page, p) for p in range(pages)])
    grid_spec = pltpu.PrefetchScalarGridSpec(
        num_scalar_prefetch=1,
        grid=(nseq, n_pages // pages),
        in_specs=in_specs,
        out_specs=[seq_spec(DA_HEADS * DA_V), seq_spec(MLA_HEADS * MLA_KV_RANK)],
        scratch_shapes=[pltpu.VMEM((rows_d, kvw), BF16), pltpu.VMEM((rows_m, 2 * LANES), BF16),
                        pltpu.VMEM((kvw, n_keys), BF16), pltpu.VMEM((DA_KV_HEADS, n_keys, DA_V), BF16),
                        pltpu.VMEM((n_keys, MLA_KV_RANK), BF16), pltpu.VMEM((MLA_ROPE, n_keys), BF16),
                        pltpu.VMEM((rows_d, 1), F32), pltpu.VMEM((rows_d, 1), F32),
                        pltpu.VMEM((rows_d, DA_V), F32),
                        pltpu.VMEM((rows_m, 1), F32), pltpu.VMEM((rows_m, 1), F32),
                        pltpu.VMEM((rows_m, MLA_KV_RANK), F32)])
    od, olat = pl.pallas_call(
        functools.partial(_sample_attn_kernel, pages=pages, page=page, dec_seq=t, out_scale=out_scale),
        grid_spec=grid_spec,
        out_shape=[jax.ShapeDtypeStruct((nseq, t, DA_HEADS * DA_V), F32),
                   jax.ShapeDtypeStruct((nseq, t, MLA_HEADS * MLA_KV_RANK), F32)],
        compiler_params=_params("parallel", "arbitrary"),
        name="sample_attention",
    )(page_table, lam, seq3(qd), seq3(qcat), seq3(kd), seq3(vd), seq3(kvcat), blast, bnew, mnew,
      g_subln.reshape(1, DA_V), *([ck] * pages), *([cv] * pages), *([cache_ckv] * pages),
      *([cr] * pages))
    return od.reshape(nseq * t, -1), olat.reshape(nseq * t, -1)


def _pack_w_in(w_in):
    qkv = DA_HEADS * DA_V + 2 * DA_KV_HEADS * DA_V + MLA_Q_RANK + MLA_KV_RANK
    pad = jnp.zeros((w_in.shape[0], LANES - MLA_ROPE), w_in.dtype)
    w_qkv = jnp.concatenate([w_in[:, :qkv + MLA_ROPE], pad], axis=1).astype(BF16)
    w_gates = w_in[:, qkv + MLA_ROPE:].astype(BF16)
    return w_qkv, w_gates


def _pack_w_qrope(w_uq):
    wq = w_uq.reshape(MLA_Q_RANK, MLA_HEADS, MLA_NOPE + MLA_ROPE)[:, :, MLA_NOPE:] * MLA_SCALE
    pad = jnp.zeros((MLA_Q_RANK, MLA_HEADS, LANES - MLA_ROPE), w_uq.dtype)
    return jnp.concatenate([wq, pad], axis=2).reshape(MLA_Q_RANK, MLA_HEADS * LANES).astype(BF16)


def _rope_tables(pos):
    half = MLA_ROPE // 2
    inv = ROPE_BASE ** (-jnp.arange(0, MLA_ROPE, 2, dtype=F32) / MLA_ROPE)
    ang = pos.astype(F32)[:, None] * inv[None, :]
    cos, sin = jnp.cos(ang), jnp.sin(ang)
    z = jnp.zeros((pos.shape[0], LANES - MLA_ROPE), F32)
    zh = jnp.zeros((pos.shape[0], half), F32)
    t_cos = jnp.concatenate([cos, cos, z], axis=1)
    t_lo = jnp.concatenate([-sin, zh, z], axis=1)
    t_hi = jnp.concatenate([zh, sin, z], axis=1)
    return tuple(a[None] for a in (t_cos, t_lo, t_hi))


def _sample_bias_rows(last, new, t, pages, page):
    heads = jnp.array([k * DA_GROUP + g for k in range(DA_KV_HEADS) for _ in range(2)
                       for g in range(DA_GROUP)], jnp.int32)
    blast = last[heads].reshape(-1, page)
    blast = jnp.concatenate([jnp.zeros((blast.shape[0], (pages - 1) * page), F32), blast], axis=1)
    bnew = new[heads].reshape(-1, new.shape[-1])
    mnew = jnp.tile(new[DA_HEADS], (MLA_HEADS, 1))
    return blast, bnew, mnew


def kernel(x_prompt, x_sample, c_prompt, c_sample, cache_diff_k, cache_diff_v, cache_mla_ckv, cache_mla_krope, page_table, rel_bias, w_ada, b_ada, g_ffn1, w_ffn1_up, w_ffn1_down, g_mix, w_in, g_q, w_uq, g_kv, w_uk, w_uv, lambda_q1, lambda_k1, lambda_q2, lambda_k2, g_subln, w_pa, w_pb, w_o, g_ffn2, w_ffn2_up, w_ffn2_down, g_final):
    depth = w_ada.shape[0]
    bp, sp, d = x_prompt.shape
    bs, ss, _ = x_sample.shape
    page = cache_diff_k.shape[2]
    past_len = page_table.shape[1] * page
    pages = min(PAGES_PER_STEP, page_table.shape[1])
    new_pad = 2 * SUBLANES
    tq = min(Q_TILE, sp)

    tiles, b_last, b_new = _bias_tiles(rel_bias, tq, page, ss, new_pad)
    blast, bnew, mnew = _sample_bias_rows(b_last, b_new, ss, pages, page)
    tab_p = _rope_tables(jnp.arange(sp, dtype=jnp.int32))
    tab_s = _rope_tables(past_len + jnp.arange(ss, dtype=jnp.int32))
    c_all = jnp.concatenate([c_prompt, c_sample], axis=0)

    xp, xs = x_prompt, x_sample
    st_p, st_s = [], []
    for l in range(depth):
        lam_init = 0.8 - 0.6 * math.exp(-0.3 * l)
        out_scale = 1.0 - lam_init
        mod = _modulation(c_all, w_ada[l], b_ada[l])
        mod_p, mod_s = mod[:bp], mod[bp:]
        w_qabs, w_uvpb, lam = _prep(w_uq[l], w_uk[l], w_uv[l], w_pb[l], lambda_q1[l], lambda_k1[l],
                                    lambda_q2[l], lambda_k2[l], lam_init)
        w_qkv, w_gates = _pack_w_in(w_in[l])
        w_qrope = _pack_w_qrope(w_uq[l])
        w1u, w1d = w_ffn1_up[l].astype(BF16), w_ffn1_down[l].astype(BF16)
        w2u, w2d = w_ffn2_up[l].astype(BF16), w_ffn2_down[l].astype(BF16)
        wpa, wo = w_pa[l].astype(BF16), w_o[l].astype(BF16)
        last = l == depth - 1

        def layer(x, mod_x, tables, attend):
            b, s, _ = x.shape
            x = _ffn(x, mod_x, 0, g_ffn1[l], w1u, w1d)
            qd, kd, kdb, vd, vaug, qcat, kvcat, ckv, kr = _mixer_in(
                x, mod_x, g_mix[l], w_qkv, g_q[l], g_kv[l], w_qabs, w_qrope, tables)
            od, olat = attend(qd, kd, kdb, vd, vaug, qcat, kvcat)
            x = _merge(x, mod_x, g_mix[l], w_gates, od, olat, wpa, w_uvpb, wo)
            x = _ffn(x, mod_x, 2, g_ffn2[l], w2u, w2d, g_final if last else None)
            new = (kd.reshape(b, s, DA_KV_HEADS, 2, DA_HEAD_DIM), vd.reshape(b, s, DA_KV_HEADS, DA_V),
                   ckv.reshape(b, s, MLA_KV_RANK), kr.reshape(b, s, MLA_ROPE))
            return x, new

        def attend_p(qd, kd, kdb, vd, vaug, qcat, kvcat):
            return _prompt_attention(lam, qd, qcat, kdb, vaug, kvcat, tiles, g_subln[l], bp, sp, out_scale)

        def attend_s(qd, kd, kdb, vd, vaug, qcat, kvcat):
            return _sample_attention(page_table, lam, qd, qcat, kd, vd, kvcat, blast, bnew, mnew,
                                     g_subln[l], cache_diff_k, cache_diff_v, cache_mla_ckv,
                                     cache_mla_krope, l, out_scale)

        xp, new_p = layer(xp, mod_p, tab_p, attend_p)
        xs, new_s = layer(xs, mod_s, tab_s, attend_s)
        st_p.append(new_p)
        st_s.append(new_s)

    stack = lambda parts: parts[0][None] if len(parts) == 1 else jnp.stack(parts)
    return (xp, xs) + tuple(stack([s[i] for s in st]) for st in (st_p, st_s) for i in range(4))
```

```python
import functools
import math

import jax
import jax.numpy as jnp
from jax import lax
from jax.experimental import pallas as pl
from jax.experimental.pallas import tpu as pltpu

F32 = jnp.float32
BF16 = jnp.bfloat16

DA_HEADS = 8
DA_KV_HEADS = 4
DA_HEAD_DIM = 64
DA_GROUP = DA_HEADS // DA_KV_HEADS
DA_V = 2 * DA_HEAD_DIM
MLA_HEADS = 8
MLA_Q_RANK = 256
MLA_KV_RANK = 128
MLA_NOPE = 64
MLA_ROPE = 32
MLA_V = 64
MLA_SCALE = 1.0 / math.sqrt(MLA_NOPE + MLA_ROPE)
LOG2E = math.log2(math.e)
ROPE_BASE = 10000.0
REL_BUCKETS = 32
REL_EXACT = REL_BUCKETS // 2
REL_MAX_DIST = 128
MACARON_WEIGHT = 0.5
N_MOD = 9
EPS = 1e-6
NEG = -1e30

LANES = 128
SUBLANES = 8
VMEM_LIMIT_BYTES = 58 * 1024 * 1024

_BUCKET_STARTS = tuple(
    int(math.ceil(REL_EXACT * (REL_MAX_DIST / REL_EXACT) ** (k / (REL_BUCKETS - REL_EXACT))))
    for k in range(1, REL_BUCKETS - REL_EXACT))

Q_TILE = 256
PAGES_PER_STEP = 8


def _rms(x, g):
    return x * lax.rsqrt(jnp.mean(x * x, axis=-1, keepdims=True) + EPS) * g


def _dot(a, b):
    return jnp.dot(a, b, preferred_element_type=F32)


def _dot_nt(a, b):
    return lax.dot_general(a, b, (((1,), (1,)), ((), ())), preferred_element_type=F32)


def _params(*semantics):
    return pltpu.CompilerParams(dimension_semantics=semantics, vmem_limit_bytes=VMEM_LIMIT_BYTES)


def _resident(shape):
    zeros = (0,) * len(shape)
    return pl.BlockSpec(shape, lambda *_: zeros, pipeline_mode=pl.Buffered(1))


def _mod_kernel(c_ref, w_ref, b_ref, o_ref):
    o_ref[...] = _dot(c_ref[...].astype(BF16), w_ref[...].astype(BF16)) + b_ref[...]


def _modulation(c, w, b):
    nb, d = c.shape
    n = w.shape[1]
    out = pl.pallas_call(
        _mod_kernel,
        grid=(n // d,),
        in_specs=[pl.BlockSpec((nb, d), lambda j: (0, 0)),
                  pl.BlockSpec((d, d), lambda j: (0, j)),
                  pl.BlockSpec((1, d), lambda j: (0, j))],
        out_specs=pl.BlockSpec((nb, d), lambda j: (0, j)),
        out_shape=jax.ShapeDtypeStruct((nb, n), F32),
        compiler_params=_params("parallel"),
        name="modulation",
    )(c, w, b.reshape(1, n))
    return out.reshape(nb, N_MOD, 1, d)


def _prep_kernel(wqn_ref, wuk_ref, wuv_ref, wpb_ref, lq1_ref, lk1_ref, lq2_ref, lk2_ref,
                 wqabs_ref, wqabst_ref, wuvpb_ref, lam_ref, *, lam_init):
    hi = lax.Precision.HIGHEST
    nt = (((1,), (1,)), ((), ()))
    for h in range(MLA_HEADS):
        qa = lax.dot_general(wqn_ref[h], wuk_ref[h], nt, precision=hi, preferred_element_type=F32)
        wqabs_ref[:, h * MLA_KV_RANK:(h + 1) * MLA_KV_RANK] = (qa * MLA_SCALE).astype(BF16)
        qat = lax.dot_general(wuk_ref[h], wqn_ref[h], nt, precision=hi, preferred_element_type=F32)
        wqabst_ref[h * MLA_KV_RANK:(h + 1) * MLA_KV_RANK, :] = (qat * (MLA_SCALE * LOG2E)).astype(BF16)
        vp = jnp.dot(wuv_ref[h], wpb_ref[h], precision=hi, preferred_element_type=F32)
        wuvpb_ref[h * MLA_KV_RANK:(h + 1) * MLA_KV_RANK, :] = vp.astype(BF16)
    s1 = jnp.sum(lq1_ref[...] * lk1_ref[...], axis=-1, keepdims=True)
    s2 = jnp.sum(lq2_ref[...] * lk2_ref[...], axis=-1, keepdims=True)
    lam_ref[...] = jnp.exp(s1) - jnp.exp(s2) + lam_init


def _prep(w_uq, w_uk, w_uv, w_pb, lq1, lk1, lq2, lk2, lam_init):
    d_model = w_pb.shape[1]
    wq = w_uq.reshape(MLA_Q_RANK, MLA_HEADS, MLA_NOPE + MLA_ROPE)
    wqn = jnp.transpose(wq[:, :, :MLA_NOPE], (1, 0, 2))
    wuk = jnp.transpose(w_uk, (1, 0, 2))
    wuv = jnp.transpose(w_uv, (1, 0, 2))
    wpb = w_pb.reshape(MLA_HEADS, MLA_V, d_model)
    row = lambda a: a.reshape(1, -1)
    return pl.pallas_call(
        functools.partial(_prep_kernel, lam_init=lam_init),
        out_shape=(jax.ShapeDtypeStruct((MLA_Q_RANK, MLA_HEADS * MLA_KV_RANK), BF16),
                   jax.ShapeDtypeStruct((MLA_HEADS * MLA_KV_RANK, MLA_Q_RANK), BF16),
                   jax.ShapeDtypeStruct((MLA_HEADS * MLA_KV_RANK, d_model), BF16),
                   jax.ShapeDtypeStruct((1, 1), F32)),
        compiler_params=pltpu.CompilerParams(vmem_limit_bytes=VMEM_LIMIT_BYTES),
        name="fold_weights",
    )(wqn, wuk, wuv, wpb, row(lq1), row(lk1), row(lq2), row(lk2))


def _bucket(n):
    bucket = jnp.full(n.shape, REL_EXACT, jnp.int32)
    for start in _BUCKET_STARTS:
        bucket = bucket + (n >= start).astype(jnp.int32)
    return jnp.where(n < REL_EXACT, n, bucket)


def _bias_rel(bucket, tab_ref, h):
    far = tab_ref[REL_BUCKETS - 1, h]
    val = jnp.zeros(bucket.shape, F32)
    for b in range(REL_BUCKETS - 1):
        val = jnp.where(bucket == b, tab_ref[b, h] - far, val)
    return val


def _bias_kernel(tab_ref, tiles_ref, last_ref, new_ref, *, tq, page, dec_seq, new_pad):
    r = lax.broadcasted_iota(jnp.int32, (tq, tq), 1)
    c = lax.broadcasted_iota(jnp.int32, (tq, tq), 0)
    for t in range(2):
        n = r - c + t * tq
        bk = _bucket(jnp.maximum(n, 0))
        for h in range(DA_HEADS):
            tiles_ref[t, h] = jnp.where(n >= 0, _bias_rel(bk, tab_ref, h) * LOG2E, NEG)
        tiles_ref[t, DA_HEADS] = jnp.where(n >= 0, 0.0, NEG)
    tr = lax.broadcasted_iota(jnp.int32, (dec_seq, page), 0)
    tc = lax.broadcasted_iota(jnp.int32, (dec_seq, page), 1)
    bk = _bucket(tr + page - tc)
    for h in range(DA_HEADS):
        last_ref[h] = _bias_rel(bk, tab_ref, h)
    ur = lax.broadcasted_iota(jnp.int32, (dec_seq, new_pad), 0)
    uc = lax.broadcasted_iota(jnp.int32, (dec_seq, new_pad), 1)
    n = ur - uc
    ok = (n >= 0) & (uc < dec_seq)
    bk = _bucket(jnp.maximum(n, 0))
    for h in range(DA_HEADS):
        new_ref[h] = jnp.where(ok, _bias_rel(bk, tab_ref, h), NEG)
    new_ref[DA_HEADS] = jnp.where(ok, 0.0, NEG)


def _bias_tiles(rel_bias, tq, page, dec_seq, new_pad):
    return pl.pallas_call(
        functools.partial(_bias_kernel, tq=tq, page=page, dec_seq=dec_seq, new_pad=new_pad),
        in_specs=[pl.BlockSpec(memory_space=pltpu.SMEM)],
        out_shape=(jax.ShapeDtypeStruct((2, DA_HEADS + 1, tq, tq), F32),
                   jax.ShapeDtypeStruct((DA_HEADS, dec_seq, page), F32),
                   jax.ShapeDtypeStruct((DA_HEADS + 1, dec_seq, new_pad), F32)),
        compiler_params=pltpu.CompilerParams(vmem_limit_bytes=VMEM_LIMIT_BYTES),
        name="bias_tiles",
    )(rel_bias)


def _token_tiling(b, s):
    if s >= 512 and s % 512 == 0:
        return 1, 512
    if s >= 256 and s % 256 == 0:
        return 1, 256
    if s >= LANES:
        assert s % LANES == 0
        return 1, LANES
    assert s % SUBLANES == 0
    bb = min(b, 256 // s)
    assert b % bb == 0
    return bb, s


def _x_spec(bb, ts, d):
    return pl.BlockSpec((bb, ts, d), lambda i, j: (i, j, 0))


def _mod_spec(bb, d, group):
    return pl.BlockSpec((bb, 3, 1, d), lambda i, j: (i, group, 0, 0))


def _rows_spec(rows, width, ns):
    return pl.BlockSpec((rows, width), lambda i, j: (i * ns + j, 0))


def _modulated(x_ref, mod_ref, g_ref):
    x = x_ref[...]
    h = _rms(x, g_ref[...]) * (1.0 + mod_ref[:, 1]) + mod_ref[:, 0]
    bb, ts, d = x.shape
    return x, h.reshape(bb * ts, d).astype(BF16)


def _ffn_kernel(x_ref, mod_ref, g_ref, wup_ref, wdn_ref, *rest, n_chunks, final_norm):
    if final_norm:
        gf_ref, o_ref = rest
    else:
        (o_ref,) = rest
    x, hb = _modulated(x_ref, mod_ref, g_ref)
    d_ff = wdn_ref.shape[0]
    cf = d_ff // n_chunks
    acc = None
    for c in range(n_chunks):
        g = _dot(hb, wup_ref[:, c * cf:(c + 1) * cf])
        u = _dot(hb, wup_ref[:, d_ff + c * cf:d_ff + (c + 1) * cf])
        a = (g * jax.nn.sigmoid(g) * u).astype(BF16)
        part = _dot(a, wdn_ref[c * cf:(c + 1) * cf, :])
        acc = part if acc is None else acc + part
    out = x + (MACARON_WEIGHT * mod_ref[:, 2]) * acc.reshape(x.shape)
    if final_norm:
        out = _rms(out, gf_ref[...])
    o_ref[...] = out


def _ffn(x, mod, group, g, w_up, w_down, g_final=None):
    b, s, d = x.shape
    bb, ts = _token_tiling(b, s)
    d_ff = w_down.shape[0]
    n_chunks = 2 if d_ff % (2 * LANES) == 0 else 1
    in_specs = [_x_spec(bb, ts, d), _mod_spec(bb, d, group), _resident((1, d)),
                _resident(w_up.shape), _resident(w_down.shape)]
    args = [x, mod, g.reshape(1, d), w_up, w_down]
    if g_final is not None:
        in_specs.append(_resident((1, d)))
        args.append(g_final.reshape(1, d))
    return pl.pallas_call(
        functools.partial(_ffn_kernel, n_chunks=n_chunks, final_norm=g_final is not None),
        grid=(b // bb, s // ts),
        in_specs=in_specs,
        out_specs=_x_spec(bb, ts, d),
        out_shape=jax.ShapeDtypeStruct(x.shape, F32),
        compiler_params=_params("parallel", "parallel"),
        name="ffn_final" if g_final is not None else "ffn",
    )(*args)


def _rope(z, cos, sin_lo, sin_hi):
    half = MLA_ROPE // 2
    return z * cos + pltpu.roll(z, LANES - half, 1) * sin_lo + pltpu.roll(z, half, 1) * sin_hi


def _mixer_in_kernel(x_ref, mod_ref, g_ref, w_ref, gq_ref, gkv_ref, wqabs_ref, wqrope_ref,
                     cos_ref, slo_ref, shi_ref,
                     qd_ref, kd_ref, kdb_ref, vd_ref, vaug_ref, qcat_ref, kvcat_ref, ckv_ref, kr_ref):
    bb, ts, d = x_ref.shape
    rows = bb * ts
    _, hb = _modulated(x_ref, mod_ref, g_ref)
    qw = DA_HEADS * DA_V
    kw = DA_KV_HEADS * DA_V
    qd_ref[...] = (_dot(hb, w_ref[:, :qw]) * (DA_HEAD_DIM ** -0.5)).astype(BF16)
    kd = _dot(hb, w_ref[:, qw:qw + kw])
    kd_ref[...] = kd
    kdb_ref[...] = kd.astype(BF16)
    vd = _dot(hb, w_ref[:, qw + kw:qw + 2 * kw])
    vd_ref[...] = vd
    ones = jnp.ones((rows, DA_V), BF16)
    for k in range(DA_KV_HEADS):
        vaug_ref[:, 2 * k * DA_V:(2 * k + 1) * DA_V] = vd[:, k * DA_V:(k + 1) * DA_V].astype(BF16)
        vaug_ref[:, (2 * k + 1) * DA_V:(2 * k + 2) * DA_V] = ones
    base = qw + 2 * kw
    lat = _dot(hb, w_ref[:, base:base + MLA_Q_RANK + MLA_KV_RANK + LANES])
    qn = _rms(lat[:, :MLA_Q_RANK], gq_ref[...]).astype(BF16)
    ckv = _rms(lat[:, MLA_Q_RANK:MLA_Q_RANK + MLA_KV_RANK], gkv_ref[...])
    kr = lat[:, MLA_Q_RANK + MLA_KV_RANK:]

    def table(ref):
        return jnp.broadcast_to(ref[...], (bb, ts, LANES)).reshape(rows, LANES)

    cos, slo, shi = table(cos_ref), table(slo_ref), table(shi_ref)
    qabs = _dot(qn, wqabs_ref[...])
    zq = _dot(qn, wqrope_ref[...])
    for h in range(MLA_HEADS):
        lo = 2 * h * LANES
        qcat_ref[:, lo:lo + LANES] = qabs[:, h * LANES:(h + 1) * LANES].astype(BF16)
        qcat_ref[:, lo + LANES:lo + 2 * LANES] = _rope(zq[:, h * LANES:(h + 1) * LANES],
                                                       cos, slo, shi).astype(BF16)
    krr = _rope(kr, cos, slo, shi)
    ckv_ref[...] = ckv
    kr_ref[...] = krr[:, :MLA_ROPE]
    lane = lax.broadcasted_iota(jnp.int32, (rows, LANES), 1)
    kvcat_ref[:, :LANES] = ckv.astype(BF16)
    kvcat_ref[:, LANES:] = jnp.where(lane == LANES - 1, 1.0, krr).astype(BF16)


def _mixer_in(x, mod, g, w_qkv, g_q, g_kv, w_qabs, w_qrope, tables):
    b, s, d = x.shape
    bb, ts = _token_tiling(b, s)
    ns = s // ts
    rows = bb * ts
    n = b * s
    widths = [(DA_HEADS * DA_V, BF16), (DA_KV_HEADS * DA_V, F32), (DA_KV_HEADS * DA_V, BF16),
              (DA_KV_HEADS * DA_V, F32), (2 * DA_KV_HEADS * DA_V, BF16),
              (2 * MLA_HEADS * LANES, BF16), (2 * LANES, BF16), (MLA_KV_RANK, F32), (MLA_ROPE, F32)]
    tab_spec = pl.BlockSpec((1, ts, LANES), lambda i, j: (0, j, 0))
    return pl.pallas_call(
        _mixer_in_kernel,
        grid=(b // bb, ns),
        in_specs=[_x_spec(bb, ts, d), _mod_spec(bb, d, 1), _resident((1, d)), _resident(w_qkv.shape),
                  _resident((1, MLA_Q_RANK)), _resident((1, MLA_KV_RANK)),
                  _resident(w_qabs.shape), _resident(w_qrope.shape), tab_spec, tab_spec, tab_spec],
        out_specs=[_rows_spec(rows, w, ns) for w, _ in widths],
        out_shape=[jax.ShapeDtypeStruct((n, w), dt) for w, dt in widths],
        compiler_params=_params("parallel", "parallel"),
        name="mixer_in",
    )(x, mod, g.reshape(1, d), w_qkv, g_q.reshape(1, -1), g_kv.reshape(1, -1), w_qabs, w_qrope, *tables)


V_AUG = DA_V + 2 * SUBLANES


def _mixer_in_prompt_kernel(x_ref, mod_ref, g_ref, wqt_ref, wk_ref, wkt_ref, wv_ref, wvt_ref, wlat_ref,
                            gq_ref, gkv_ref, wqabst_ref, wqropet_ref, cos_ref, slo_ref, shi_ref,
                            cost_ref, sint_ref,
                            qdt_ref, kdt_ref, kdb_ref, v4_ref, vta_ref, qcatt_ref, kvcat_ref, ckv_ref,
                            cta_ref, krt_ref, *, tk):
    _, ts, _ = x_ref.shape
    _, hb = _modulated(x_ref, mod_ref, g_ref)
    half = MLA_ROPE // 2
    qdt_ref[0] = (_dot_nt(wqt_ref[...], hb) * (DA_HEAD_DIM ** -0.5 * LOG2E)).astype(BF16)
    kdt_ref[0] = _dot_nt(wkt_ref[...], hb)
    kdb_ref[...] = _dot(hb, wk_ref[...]).astype(BF16)
    vd = _dot(hb, wv_ref[...])
    for k in range(DA_KV_HEADS):
        v4_ref[pl.ds(k, ts, stride=DA_KV_HEADS), :] = vd[:, k * DA_V:(k + 1) * DA_V]
    vt = _dot_nt(wvt_ref[...], hb)
    ones = jnp.ones((V_AUG - DA_V, tk), BF16)
    for c in range(ts // tk):
        cols = slice(c * tk, (c + 1) * tk)
        for k in range(DA_KV_HEADS):
            vta_ref[0, c, k * V_AUG:k * V_AUG + DA_V, :] = vt[k * DA_V:(k + 1) * DA_V, cols].astype(BF16)
            vta_ref[0, c, k * V_AUG + DA_V:(k + 1) * V_AUG, :] = ones
    lat = _dot(hb, wlat_ref[...])
    qn = _rms(lat[:, :MLA_Q_RANK], gq_ref[...]).astype(BF16)
    ckv = _rms(lat[:, MLA_Q_RANK:MLA_Q_RANK + MLA_KV_RANK], gkv_ref[...])
    krr = _rope(lat[:, MLA_Q_RANK + MLA_KV_RANK:], cos_ref[0], slo_ref[0], shi_ref[0])
    ckv_ref[...] = ckv
    kvcat_ref[:, :LANES] = ckv.astype(BF16)
    kvcat_ref[:, LANES:] = krr.astype(BF16)
    krt_ref[0] = krr.T[:MLA_ROPE, :]
    ckvt = ckv.T
    for c in range(ts // tk):
        cols = slice(c * tk, (c + 1) * tk)
        cta_ref[0, c, :MLA_KV_RANK, :] = ckvt[:, cols].astype(BF16)
        cta_ref[0, c, MLA_KV_RANK:, :] = ones
    qabst = _dot_nt(wqabst_ref[...], qn)
    zqt = _dot_nt(wqropet_ref[...], qn)
    cost, sint = cost_ref[...], sint_ref[...]
    zeros = jnp.zeros((2 * LANES - MLA_KV_RANK - MLA_ROPE, ts), BF16)
    for h in range(MLA_HEADS):
        lo = 2 * h * LANES
        x1 = zqt[h * MLA_ROPE:h * MLA_ROPE + half]
        x2 = zqt[h * MLA_ROPE + half:(h + 1) * MLA_ROPE]
        qcatt_ref[0, lo:lo + MLA_KV_RANK, :] = qabst[h * MLA_KV_RANK:(h + 1) * MLA_KV_RANK].astype(BF16)
        r0 = lo + MLA_KV_RANK
        qcatt_ref[0, r0:r0 + half, :] = (x1 * cost - x2 * sint).astype(BF16)
        qcatt_ref[0, r0 + half:r0 + MLA_ROPE, :] = (x1 * sint + x2 * cost).astype(BF16)
        qcatt_ref[0, r0 + MLA_ROPE:lo + 2 * LANES, :] = zeros


def _mixer_in_prompt(x, mod, g, wt, g_q, g_kv, w_qabst, w_qropet, tables, tables_t, tk):
    b, s, d = x.shape
    bb, ts = _token_tiling(b, s)
    assert bb == 1 and ts % tk == 0
    ns = s // ts
    nk = s // tk
    n = b * s
    kw = DA_KV_HEADS * DA_V
    feat = lambda w: pl.BlockSpec((1, w, ts), lambda i, j: (i, 0, j))
    blocked = lambda w: pl.BlockSpec((1, ts // tk, w, tk), lambda i, j: (i, j, 0, 0))
    rows = lambda w, mult=1: pl.BlockSpec((ts * mult, w), lambda i, j: (i * ns + j, 0))
    tab_spec = pl.BlockSpec((1, ts, LANES), lambda i, j: (0, j, 0))
    tabt_spec = pl.BlockSpec((MLA_ROPE // 2, ts), lambda i, j: (0, j))
    out_specs = [feat(DA_HEADS * DA_V), feat(kw), rows(kw), rows(DA_V, DA_KV_HEADS),
                 blocked(DA_KV_HEADS * V_AUG), feat(2 * MLA_HEADS * LANES), rows(2 * LANES),
                 rows(MLA_KV_RANK), blocked(V_AUG), feat(MLA_ROPE)]
    out_shape = [jax.ShapeDtypeStruct((b, DA_HEADS * DA_V, s), BF16),
                 jax.ShapeDtypeStruct((b, kw, s), F32),
                 jax.ShapeDtypeStruct((n, kw), BF16),
                 jax.ShapeDtypeStruct((n * DA_KV_HEADS, DA_V), F32),
                 jax.ShapeDtypeStruct((b, nk, DA_KV_HEADS * V_AUG, tk), BF16),
                 jax.ShapeDtypeStruct((b, 2 * MLA_HEADS * LANES, s), BF16),
                 jax.ShapeDtypeStruct((n, 2 * LANES), BF16),
                 jax.ShapeDtypeStruct((n, MLA_KV_RANK), F32),
                 jax.ShapeDtypeStruct((b, nk, V_AUG, tk), BF16),
                 jax.ShapeDtypeStruct((b, MLA_ROPE, s), F32)]
    wqt, wk, wkt, wv, wvt, wlat = wt
    return pl.pallas_call(
        functools.partial(_mixer_in_prompt_kernel, tk=tk),
        grid=(b, ns),
        in_specs=[_x_spec(bb, ts, d), _mod_spec(bb, d, 1), _resident((1, d)),
                  _resident(wqt.shape), _resident(wk.shape), _resident(wkt.shape), _resident(wv.shape),
                  _resident(wvt.shape), _resident(wlat.shape),
                  _resident((1, MLA_Q_RANK)), _resident((1, MLA_KV_RANK)),
                  _resident(w_qabst.shape), _resident(w_qropet.shape),
                  tab_spec, tab_spec, tab_spec, tabt_spec, tabt_spec],
        out_specs=out_specs,
        out_shape=out_shape,
        compiler_params=_params("parallel", "parallel"),
        name="mixer_in_prompt",
    )(x, mod, g.reshape(1, d), wqt, wk, wkt, wv, wvt, wlat, g_q.reshape(1, -1), g_kv.reshape(1, -1),
      w_qabst, w_qropet, *tables, *tables_t)


def _merge_kernel(x_ref, mod_ref, g_ref, wg_ref, od_ref, olat_ref, wpa_ref, wuvpb_ref, wo_ref, o_ref):
    x, hb = _modulated(x_ref, mod_ref, g_ref)
    d = x.shape[-1]
    ya = _dot(od_ref[...].astype(BF16), wpa_ref[...])
    yb = _dot(olat_ref[...].astype(BF16), wuvpb_ref[...])
    ga = jax.nn.sigmoid(_dot(hb, wg_ref[:, :d]))
    gb = jax.nn.sigmoid(_dot(hb, wg_ref[:, d:]))
    mix = (ga * ya + gb * yb).astype(BF16)
    o_ref[...] = x + mod_ref[:, 2] * _dot(mix, wo_ref[...]).reshape(x.shape)


def _merge(x, mod, g, w_gates, od, olat, w_pa, w_uvpb, w_o):
    b, s, d = x.shape
    bb, ts = _token_tiling(b, s)
    ns = s // ts
    rows = bb * ts
    return pl.pallas_call(
        _merge_kernel,
        grid=(b // bb, ns),
        in_specs=[_x_spec(bb, ts, d), _mod_spec(bb, d, 1), _resident((1, d)), _resident(w_gates.shape),
                  _rows_spec(rows, od.shape[1], ns), _rows_spec(rows, olat.shape[1], ns),
                  _resident(w_pa.shape), _resident(w_uvpb.shape), _resident(w_o.shape)],
        out_specs=_x_spec(bb, ts, d),
        out_shape=jax.ShapeDtypeStruct(x.shape, F32),
        compiler_params=_params("parallel", "parallel"),
        name="merge",
    )(x, mod, g.reshape(1, d), w_gates, od, olat, w_pa, w_uvpb, w_o)


def _diff_out(o0, l0, o1, l1, lam, g_subln, out_scale):
    o = o0 / l0 - lam * (o1 / l1)
    return _rms(o, g_subln) * out_scale


def _half_masks(rows):
    lane = lax.broadcasted_iota(jnp.int32, (rows, DA_V), 1)
    return lane < DA_HEAD_DIM


N_DIFF_CHAINS = 2 * DA_HEADS
N_CHAINS = N_DIFF_CHAINS + MLA_HEADS
PIPE_DELAYS = (2, 3, 4, 6)


def _prompt_attn_kernel(lam_ref, qdt_ref, qcatt_ref, kdb_ref, vta_ref, kvcat_ref, cta_ref, tiles_ref,
                        gs_ref, od_ref, olat_ref, qm_scr, m_scr, acc_scr, *, tq, out_scale):
    i = pl.program_id(1)
    lam = lam_ref[0, 0]
    near_lo = jnp.maximum(i - 1, 0)

    zero = jnp.zeros((DA_HEAD_DIM, tq), BF16)
    for hq in range(DA_HEADS):
        q = qdt_ref[0, hq * DA_V:(hq + 1) * DA_V, :]
        qm_scr[2 * hq] = jnp.concatenate([q[:DA_HEAD_DIM], zero], axis=0)
        qm_scr[2 * hq + 1] = jnp.concatenate([zero, q[DA_HEAD_DIM:]], axis=0)
    m_scr[...] = jnp.full(m_scr.shape, -jnp.inf, F32)
    acc_scr[...] = jnp.zeros(acc_scr.shape, F32)

    def block(j, near):
        rows = pl.ds(pl.multiple_of(j * tq, tq), tq)
        t = i - j

        def operands(c):
            if c < N_DIFF_CHAINS:
                hq = c // 2
                k = hq // DA_GROUP
                return (kdb_ref[0, rows, k * DA_V:(k + 1) * DA_V], qm_scr[c],
                        vta_ref[0, j, k * V_AUG:(k + 1) * V_AUG, :], tiles_ref[t, hq] if near else None)
            h = c - N_DIFF_CHAINS
            return (kvcat_ref[0, rows, :], qcatt_ref[0, 2 * h * LANES:(2 * h + 2) * LANES, :],
                    cta_ref[0, j], tiles_ref[t, DA_HEADS] if near else None)

        def scores(c):
            keys, q, _, bias = operands(c)
            s = _dot(keys, q)
            return s if bias is None else s + bias

        def running_max(c, s):
            m_old = m_scr[c]
            m_new = jnp.maximum(m_old, jnp.max(s, axis=0, keepdims=True))
            m_scr[c] = m_new
            return s, m_new, jnp.exp2(m_old - m_new)

        def probs(s, m_new, alpha):
            return jnp.exp2(s - m_new).astype(BF16), alpha

        def weigh(c, p, alpha):
            return _dot(operands(c)[2], p), alpha

        def accumulate(c, pv, alpha):
            acc_scr[c] = acc_scr[c] * alpha + pv

        stages = [(0, lambda c, _: scores(c)), (PIPE_DELAYS[0], running_max),
                  (PIPE_DELAYS[1], lambda c, v: probs(*v)), (PIPE_DELAYS[2], lambda c, v: weigh(c, *v)),
                  (PIPE_DELAYS[3], lambda c, v: accumulate(c, *v))]
        live = {}
        for n in range(N_CHAINS + PIPE_DELAYS[-1]):
            for d, stage in stages:
                c = n - d
                if 0 <= c < N_CHAINS:
                    live[c] = stage(c, live.get(c))

    def far_body(j, carry):
        block(j, False)
        return carry

    def near_body(j, carry):
        block(j, True)
        return carry

    lax.fori_loop(0, near_lo, far_body, 0)
    lax.fori_loop(near_lo, i + 1, near_body, 0)

    gs = gs_ref[...]
    for hq in range(DA_HEADS):
        a0 = acc_scr[2 * hq]
        a1 = acc_scr[2 * hq + 1]
        o = a0[:DA_V] / a0[DA_V:DA_V + 1] - lam * (a1[:DA_V] / a1[DA_V:DA_V + 1])
        o = o * lax.rsqrt(jnp.mean(o * o, axis=0, keepdims=True) + EPS) * gs * out_scale
        od_ref[0, :, hq * DA_V:(hq + 1) * DA_V] = o.T.astype(BF16)
    for h in range(MLA_HEADS):
        a = acc_scr[N_DIFF_CHAINS + h]
        olat_ref[0, :, h * LANES:(h + 1) * LANES] = (a[:MLA_KV_RANK] / a[MLA_KV_RANK:MLA_KV_RANK + 1]).T.astype(BF16)


def _prompt_attention(lam, qdt, qcatt, kdb, vta, kvcat, cta, tiles, g_subln, b, s, out_scale):
    tq = tiles.shape[-1]
    assert s % tq == 0 and vta.shape[-1] == tq
    nk = s // tq
    v3 = lambda a: a.reshape(b, s, a.shape[-1])
    qspec = lambda w: pl.BlockSpec((1, w, tq), lambda bi, i: (bi, 0, i))
    ospec = lambda w: pl.BlockSpec((1, tq, w), lambda bi, i: (bi, i, 0))
    kvspec = lambda w: pl.BlockSpec((1, s, w), lambda bi, i: (bi, 0, 0))
    blocked = lambda w: pl.BlockSpec((1, nk, w, tq), lambda bi, i: (bi, 0, 0, 0))
    od, olat = pl.pallas_call(
        functools.partial(_prompt_attn_kernel, tq=tq, out_scale=out_scale),
        grid=(b, s // tq),
        in_specs=[pl.BlockSpec(memory_space=pltpu.SMEM),
                  qspec(qdt.shape[1]), qspec(qcatt.shape[1]),
                  kvspec(kdb.shape[-1]), blocked(vta.shape[2]), kvspec(kvcat.shape[-1]),
                  blocked(cta.shape[2]), _resident(tiles.shape), _resident((DA_V, 1))],
        out_specs=[ospec(DA_HEADS * DA_V), ospec(MLA_HEADS * MLA_KV_RANK)],
        out_shape=[jax.ShapeDtypeStruct((b, s, DA_HEADS * DA_V), BF16),
                   jax.ShapeDtypeStruct((b, s, MLA_HEADS * MLA_KV_RANK), BF16)],
        scratch_shapes=[pltpu.VMEM((N_DIFF_CHAINS, DA_V, tq), BF16),
                        pltpu.VMEM((N_CHAINS, 1, tq), F32),
                        pltpu.VMEM((N_CHAINS, V_AUG, tq), F32)],
        compiler_params=_params("parallel", "arbitrary"),
        name="prompt_attention",
    )(lam, qdt, qcatt, v3(kdb), vta, v3(kvcat), cta, tiles, g_subln.reshape(DA_V, 1))
    return od.reshape(b * s, -1), olat.reshape(b * s, -1)


def _sample_attn_kernel(pt_ref, lam_ref, qd_ref, qcat_ref, kn_ref, vn_ref, kvn_ref,
                        blast_ref, bnew_ref, mnew_ref, gs_ref, *rest, pages, page, dec_seq, out_scale):
    del pt_ref
    k_refs = rest[:pages]
    v_refs = rest[pages:2 * pages]
    c_refs = rest[2 * pages:3 * pages]
    r_refs = rest[3 * pages:4 * pages]
    od_ref, olat_ref = rest[4 * pages:4 * pages + 2]
    (qbd_scr, qm_scr, kb_scr, vb_scr, cb_scr, rb_scr,
     md_scr, ld_scr, ad_scr, mm_scr, lm_scr, am_scr) = rest[4 * pages + 2:]
    g = pl.program_id(1)
    ng = pl.num_programs(1)
    t = dec_seq

    @pl.when(g == 0)
    def _():
        qd = qd_ref[0]
        first = _half_masks(t)
        zero = jnp.zeros((t, DA_V), F32)
        blocks = []
        for k in range(DA_KV_HEADS):
            for m in range(2):
                for gg in range(DA_GROUP):
                    hq = k * DA_GROUP + gg
                    keep = first if m == 0 else jnp.logical_not(first)
                    piece = jnp.where(keep, qd[:, hq * DA_V:(hq + 1) * DA_V], 0.0)
                    blocks.append(jnp.concatenate(
                        [piece if kk == k else zero for kk in range(DA_KV_HEADS)], axis=1))
        qbd_scr[...] = jnp.concatenate(blocks, axis=0).astype(BF16)
        qc = qcat_ref[0]
        qm_scr[...] = jnp.concatenate(
            [qc[:, 2 * h * LANES:(2 * h + 2) * LANES] for h in range(MLA_HEADS)], axis=0).astype(BF16)
        md_scr[...] = jnp.full(md_scr.shape, -jnp.inf, F32)
        mm_scr[...] = jnp.full(mm_scr.shape, -jnp.inf, F32)
        ld_scr[...] = jnp.zeros(ld_scr.shape, F32)
        lm_scr[...] = jnp.zeros(lm_scr.shape, F32)
        ad_scr[...] = jnp.zeros(ad_scr.shape, F32)
        am_scr[...] = jnp.zeros(am_scr.shape, F32)

    for p in range(pages):
        keys = slice(p * page, (p + 1) * page)
        kb_scr[:, keys] = k_refs[p][0, 0].astype(BF16)
        for k in range(DA_KV_HEADS):
            vb_scr[k, keys, :] = v_refs[p][0, 0, pl.ds(k, page, stride=DA_KV_HEADS), :].astype(BF16)
        cb_scr[keys, :] = c_refs[p][0, 0].astype(BF16)
        rb_scr[:, keys] = r_refs[p][0, 0].astype(BF16)

    def update_diff(s, vals):
        m_old = md_scr[...]
        m_new = jnp.maximum(m_old, jnp.max(s, axis=-1, keepdims=True))
        pr = jnp.exp(s - m_new)
        alpha = jnp.exp(m_old - m_new)
        md_scr[...] = m_new
        ld_scr[...] = alpha * ld_scr[...] + jnp.sum(pr, axis=-1, keepdims=True)
        pb = pr.astype(BF16)
        rk = 2 * DA_GROUP * t
        for k in range(DA_KV_HEADS):
            pv = _dot(pb[k * rk:(k + 1) * rk], vals(k))
            ad_scr[k * rk:(k + 1) * rk, :] = alpha[k * rk:(k + 1) * rk] * ad_scr[k * rk:(k + 1) * rk, :] + pv

    def update_mla(s, vals):
        m_old = mm_scr[...]
        m_new = jnp.maximum(m_old, jnp.max(s, axis=-1, keepdims=True))
        pr = jnp.exp(s - m_new)
        alpha = jnp.exp(m_old - m_new)
        mm_scr[...] = m_new
        lm_scr[...] = alpha * lm_scr[...] + jnp.sum(pr, axis=-1, keepdims=True)
        am_scr[...] = alpha * am_scr[...] + _dot(pr.astype(BF16), vals)

    is_last = (g == ng - 1).astype(F32)
    s_d = _dot(qbd_scr[...], kb_scr[...]) + is_last * blast_ref[...]
    update_diff(s_d, lambda k: vb_scr[k])
    s_m = (_dot_nt(qm_scr[:, :MLA_KV_RANK], cb_scr[...])
           + _dot(qm_scr[:, MLA_KV_RANK:MLA_KV_RANK + MLA_ROPE], rb_scr[...]))
    update_mla(s_m, cb_scr[...])

    @pl.when(g == ng - 1)
    def _():
        pad = bnew_ref.shape[-1] - t
        zpad = lambda a: jnp.concatenate([a, jnp.zeros((pad, a.shape[-1]), a.dtype)], axis=0)
        kn = zpad(kn_ref[0]).astype(BF16)
        vn = zpad(vn_ref[0]).astype(BF16)
        kvn = zpad(kvn_ref[0]).astype(BF16)
        update_diff(_dot_nt(qbd_scr[...], kn) + bnew_ref[...], lambda k: vn[:, k * DA_V:(k + 1) * DA_V])
        update_mla(_dot_nt(qm_scr[...], kvn) + mnew_ref[...], kvn[:, :MLA_KV_RANK])
        lam = lam_ref[0, 0]
        for k in range(DA_KV_HEADS):
            for gg in range(DA_GROUP):
                r0 = ((k * 2 + 0) * DA_GROUP + gg) * t
                r1 = ((k * 2 + 1) * DA_GROUP + gg) * t
                o = _diff_out(ad_scr[r0:r0 + t, :], ld_scr[r0:r0 + t, :],
                              ad_scr[r1:r1 + t, :], ld_scr[r1:r1 + t, :], lam, gs_ref[...], out_scale)
                hq = k * DA_GROUP + gg
                od_ref[0, :, hq * DA_V:(hq + 1) * DA_V] = o
        for h in range(MLA_HEADS):
            olat_ref[0, :, h * LANES:(h + 1) * LANES] = (
                am_scr[h * t:(h + 1) * t, :] / lm_scr[h * t:(h + 1) * t, :])


def _sample_attention(page_table, lam, qd, qcat, kd, vd, kvcat, blast, bnew, mnew, g_subln,
                      cache_k, cache_v, cache_ckv, cache_kr, layer, out_scale):
    nseq, n_pages = page_table.shape
    page = cache_k.shape[2]
    t = qd.shape[0] // nseq
    pages = min(PAGES_PER_STEP, n_pages)
    assert n_pages % pages == 0
    n_keys = pages * page
    kvw = DA_KV_HEADS * DA_V
    depth, n_pool = cache_k.shape[:2]
    ck = jnp.transpose(cache_k, (0, 1, 3, 4, 5, 2)).reshape(depth, n_pool, kvw, page)
    cv = cache_v.reshape(depth, n_pool, page * DA_KV_HEADS, DA_V)
    cr = jnp.transpose(cache_kr, (0, 1, 3, 2))
    seq3 = lambda a: a.reshape(nseq, t, a.shape[-1]).astype(F32)
    seq_spec = lambda w: pl.BlockSpec((1, t, w), lambda b, g, pt: (b, 0, 0))
    const = lambda shape: pl.BlockSpec(shape, lambda b, g, pt: (0,) * len(shape))

    def page_spec(rows, width, p):
        return pl.BlockSpec((1, 1, rows, width), lambda b, g, pt: (layer, pt[b, g * pages + p], 0, 0))

    rows_d = DA_KV_HEADS * 2 * DA_GROUP * t
    rows_m = MLA_HEADS * t
    in_specs = ([pl.BlockSpec(memory_space=pltpu.SMEM),
                 seq_spec(DA_HEADS * DA_V), seq_spec(2 * MLA_HEADS * LANES), seq_spec(kvw), seq_spec(kvw),
                 seq_spec(2 * LANES), const(blast.shape), const(bnew.shape), const(mnew.shape),
                 const((1, DA_V))]
                + [page_spec(kvw, page, p) for p in range(pages)]
                + [page_spec(page * DA_KV_HEADS, DA_V, p) for p in range(pages)]
                + [page_spec(page, MLA_KV_RANK, p) for p in range(pages)]
                + [page_spec(MLA_ROPE, page, p) for p in range(pages)])
    grid_spec = pltpu.PrefetchScalarGridSpec(
        num_scalar_prefetch=1,
        grid=(nseq, n_pages // pages),
        in_specs=in_specs,
        out_specs=[seq_spec(DA_HEADS * DA_V), seq_spec(MLA_HEADS * MLA_KV_RANK)],
        scratch_shapes=[pltpu.VMEM((rows_d, kvw), BF16), pltpu.VMEM((rows_m, 2 * LANES), BF16),
                        pltpu.VMEM((kvw, n_keys), BF16), pltpu.VMEM((DA_KV_HEADS, n_keys, DA_V), BF16),
                        pltpu.VMEM((n_keys, MLA_KV_RANK), BF16), pltpu.VMEM((MLA_ROPE, n_keys), BF16),
                        pltpu.VMEM((rows_d, 1), F32), pltpu.VMEM((rows_d, 1), F32),
                        pltpu.VMEM((rows_d, DA_V), F32),
                        pltpu.VMEM((rows_m, 1), F32), pltpu.VMEM((rows_m, 1), F32),
                        pltpu.VMEM((rows_m, MLA_KV_RANK), F32)])
    od, olat = pl.pallas_call(
        functools.partial(_sample_attn_kernel, pages=pages, page=page, dec_seq=t, out_scale=out_scale),
        grid_spec=grid_spec,
        out_shape=[jax.ShapeDtypeStruct((nseq, t, DA_HEADS * DA_V), F32),
                   jax.ShapeDtypeStruct((nseq, t, MLA_HEADS * MLA_KV_RANK), F32)],
        compiler_params=_params("parallel", "arbitrary"),
        name="sample_attention",
    )(page_table, lam, seq3(qd), seq3(qcat), seq3(kd), seq3(vd), seq3(kvcat), blast, bnew, mnew,
      g_subln.reshape(1, DA_V), *([ck] * pages), *([cv] * pages), *([cache_ckv] * pages),
      *([cr] * pages))
    return od.reshape(nseq * t, -1), olat.reshape(nseq * t, -1)


def _pack_w_in(w_in):
    qkv = DA_HEADS * DA_V + 2 * DA_KV_HEADS * DA_V + MLA_Q_RANK + MLA_KV_RANK
    pad = jnp.zeros((w_in.shape[0], LANES - MLA_ROPE), w_in.dtype)
    w_qkv = jnp.concatenate([w_in[:, :qkv + MLA_ROPE], pad], axis=1).astype(BF16)
    w_gates = w_in[:, qkv + MLA_ROPE:].astype(BF16)
    return w_qkv, w_gates


def _pack_w_qrope(w_uq):
    wq = w_uq.reshape(MLA_Q_RANK, MLA_HEADS, MLA_NOPE + MLA_ROPE)[:, :, MLA_NOPE:] * MLA_SCALE
    pad = jnp.zeros((MLA_Q_RANK, MLA_HEADS, LANES - MLA_ROPE), w_uq.dtype)
    return jnp.concatenate([wq, pad], axis=2).reshape(MLA_Q_RANK, MLA_HEADS * LANES).astype(BF16)


def _pack_w_prompt(w_in):
    qw = DA_HEADS * DA_V
    kw = DA_KV_HEADS * DA_V
    wq, wk, wv = w_in[:, :qw], w_in[:, qw:qw + kw], w_in[:, qw + kw:qw + 2 * kw]
    base = qw + 2 * kw
    pad = jnp.zeros((w_in.shape[0], LANES - MLA_ROPE), w_in.dtype)
    wlat = jnp.concatenate([w_in[:, base:base + MLA_Q_RANK + MLA_KV_RANK + MLA_ROPE], pad], axis=1)
    return tuple(a.astype(BF16) for a in (wq.T, wk, wk.T, wv, wv.T, wlat))


def _pack_w_qrope_t(w_uq):
    wq = w_uq.reshape(MLA_Q_RANK, MLA_HEADS, MLA_NOPE + MLA_ROPE)[:, :, MLA_NOPE:] * (MLA_SCALE * LOG2E)
    return jnp.transpose(wq, (1, 2, 0)).reshape(MLA_HEADS * MLA_ROPE, MLA_Q_RANK).astype(BF16)


def _rope_tables_t(pos):
    inv = ROPE_BASE ** (-jnp.arange(0, MLA_ROPE, 2, dtype=F32) / MLA_ROPE)
    ang = inv[:, None] * pos.astype(F32)[None, :]
    return jnp.cos(ang), jnp.sin(ang)


def _rope_tables(pos):
    half = MLA_ROPE // 2
    inv = ROPE_BASE ** (-jnp.arange(0, MLA_ROPE, 2, dtype=F32) / MLA_ROPE)
    ang = pos.astype(F32)[:, None] * inv[None, :]
    cos, sin = jnp.cos(ang), jnp.sin(ang)
    z = jnp.zeros((pos.shape[0], LANES - MLA_ROPE), F32)
    zh = jnp.zeros((pos.shape[0], half), F32)
    t_cos = jnp.concatenate([cos, cos, z], axis=1)
    t_lo = jnp.concatenate([-sin, zh, z], axis=1)
    t_hi = jnp.concatenate([zh, sin, z], axis=1)
    return tuple(a[None] for a in (t_cos, t_lo, t_hi))


def _sample_bias_rows(last, new, t, pages, page):
    heads = jnp.array([k * DA_GROUP + g for k in range(DA_KV_HEADS) for _ in range(2)
                       for g in range(DA_GROUP)], jnp.int32)
    blast = last[heads].reshape(-1, page)
    blast = jnp.concatenate([jnp.zeros((blast.shape[0], (pages - 1) * page), F32), blast], axis=1)
    bnew = new[heads].reshape(-1, new.shape[-1])
    mnew = jnp.tile(new[DA_HEADS], (MLA_HEADS, 1))
    return blast, bnew, mnew


def kernel(x_prompt, x_sample, c_prompt, c_sample, cache_diff_k, cache_diff_v, cache_mla_ckv, cache_mla_krope, page_table, rel_bias, w_ada, b_ada, g_ffn1, w_ffn1_up, w_ffn1_down, g_mix, w_in, g_q, w_uq, g_kv, w_uk, w_uv, lambda_q1, lambda_k1, lambda_q2, lambda_k2, g_subln, w_pa, w_pb, w_o, g_ffn2, w_ffn2_up, w_ffn2_down, g_final):
    depth = w_ada.shape[0]
    bp, sp, d = x_prompt.shape
    bs, ss, _ = x_sample.shape
    page = cache_diff_k.shape[2]
    past_len = page_table.shape[1] * page
    pages = min(PAGES_PER_STEP, page_table.shape[1])
    new_pad = 2 * SUBLANES
    tq = min(Q_TILE, sp)

    tiles, b_last, b_new = _bias_tiles(rel_bias, tq, page, ss, new_pad)
    blast, bnew, mnew = _sample_bias_rows(b_last, b_new, ss, pages, page)
    tab_p = _rope_tables(jnp.arange(sp, dtype=jnp.int32))
    tabt_p = _rope_tables_t(jnp.arange(sp, dtype=jnp.int32))
    tab_s = _rope_tables(past_len + jnp.arange(ss, dtype=jnp.int32))
    c_all = jnp.concatenate([c_prompt, c_sample], axis=0)

    xp, xs = x_prompt, x_sample
    st_p, st_s = [], []
    for l in range(depth):
        lam_init = 0.8 - 0.6 * math.exp(-0.3 * l)
        out_scale = 1.0 - lam_init
        mod = _modulation(c_all, w_ada[l], b_ada[l])
        mod_p, mod_s = mod[:bp], mod[bp:]
        w_qabs, w_qabst, w_uvpb, lam = _prep(w_uq[l], w_uk[l], w_uv[l], w_pb[l], lambda_q1[l],
                                             lambda_k1[l], lambda_q2[l], lambda_k2[l], lam_init)
        w_qkv, w_gates = _pack_w_in(w_in[l])
        w_prompt = _pack_w_prompt(w_in[l])
        w_qrope = _pack_w_qrope(w_uq[l])
        w_qropet = _pack_w_qrope_t(w_uq[l])
        w1u, w1d = w_ffn1_up[l].astype(BF16), w_ffn1_down[l].astype(BF16)
        w2u, w2d = w_ffn2_up[l].astype(BF16), w_ffn2_down[l].astype(BF16)
        wpa, wo = w_pa[l].astype(BF16), w_o[l].astype(BF16)
        g_last = g_final if l == depth - 1 else None

        xp = _ffn(xp, mod_p, 0, g_ffn1[l], w1u, w1d)
        qdt, kdt, kdb, v4, vta, qcatt, kvcat, ckv, cta, krt = _mixer_in_prompt(
            xp, mod_p, g_mix[l], w_prompt, g_q[l], g_kv[l], w_qabst, w_qropet, tab_p, tabt_p, tq)
        od, olat = _prompt_attention(lam, qdt, qcatt, kdb, vta, kvcat, cta, tiles, g_subln[l], bp, sp,
                                     out_scale)
        xp = _merge(xp, mod_p, g_mix[l], w_gates, od, olat, wpa, w_uvpb, wo)
        xp = _ffn(xp, mod_p, 2, g_ffn2[l], w2u, w2d, g_last)
        st_p.append((jnp.transpose(kdt.reshape(bp, DA_KV_HEADS, 2, DA_HEAD_DIM, sp), (0, 4, 1, 2, 3)),
                     v4.reshape(bp, sp, DA_KV_HEADS, DA_V),
                     ckv.reshape(bp, sp, MLA_KV_RANK),
                     jnp.transpose(krt, (0, 2, 1))))

        xs = _ffn(xs, mod_s, 0, g_ffn1[l], w1u, w1d)
        qd, kd, _, vd, _, qcat, kvc, ckv_s, kr_s = _mixer_in(
            xs, mod_s, g_mix[l], w_qkv, g_q[l], g_kv[l], w_qabs, w_qrope, tab_s)
        od, olat = _sample_attention(page_table, lam, qd, qcat, kd, vd, kvc, blast, bnew, mnew,
                                     g_subln[l], cache_diff_k, cache_diff_v, cache_mla_ckv,
                                     cache_mla_krope, l, out_scale)
        xs = _merge(xs, mod_s, g_mix[l], w_gates, od, olat, wpa, w_uvpb, wo)
        xs = _ffn(xs, mod_s, 2, g_ffn2[l], w2u, w2d, g_last)
        st_s.append((kd.reshape(bs, ss, DA_KV_HEADS, 2, DA_HEAD_DIM), vd.reshape(bs, ss, DA_KV_HEADS, DA_V),
                     ckv_s.reshape(bs, ss, MLA_KV_RANK), kr_s.reshape(bs, ss, MLA_ROPE)))

    stack = lambda parts: parts[0][None] if len(parts) == 1 else jnp.stack(parts)
    return (xp, xs) + tuple(stack([s[i] for s in st]) for st in (st_p, st_s) for i in range(4))
```

```python
import functools
import math

import jax
import jax.numpy as jnp
from jax import lax
from jax.experimental import pallas as pl
from jax.experimental.pallas import tpu as pltpu

F32 = jnp.float32
BF16 = jnp.bfloat16

DA_HEADS = 8
DA_KV_HEADS = 4
DA_HEAD_DIM = 64
DA_GROUP = DA_HEADS // DA_KV_HEADS
DA_V = 2 * DA_HEAD_DIM
MLA_HEADS = 8
MLA_Q_RANK = 256
MLA_KV_RANK = 128
MLA_NOPE = 64
MLA_ROPE = 32
MLA_V = 64
MLA_SCALE = 1.0 / math.sqrt(MLA_NOPE + MLA_ROPE)
LOG2E = math.log2(math.e)
ROPE_BASE = 10000.0
REL_BUCKETS = 32
REL_EXACT = REL_BUCKETS // 2
REL_MAX_DIST = 128
MACARON_WEIGHT = 0.5
N_MOD = 9
EPS = 1e-6
NEG = -1e30

LANES = 128
SUBLANES = 8
MXU_WIDTH = 256
VMEM_LIMIT_BYTES = 58 * 1024 * 1024

_BUCKET_STARTS = tuple(
    int(math.ceil(REL_EXACT * (REL_MAX_DIST / REL_EXACT) ** (k / (REL_BUCKETS - REL_EXACT))))
    for k in range(1, REL_BUCKETS - REL_EXACT))

Q_TILE = 256
PAGES_PER_STEP = 8


def _rms(x, g):
    return x * lax.rsqrt(jnp.mean(x * x, axis=-1, keepdims=True) + EPS) * g


def _dot(a, b):
    return jnp.dot(a, b, preferred_element_type=F32)


def _dot_nt(a, b):
    return lax.dot_general(a, b, (((1,), (1,)), ((), ())), preferred_element_type=F32)


def _params(*semantics):
    return pltpu.CompilerParams(dimension_semantics=semantics, vmem_limit_bytes=VMEM_LIMIT_BYTES)


def _resident(shape):
    zeros = (0,) * len(shape)
    return pl.BlockSpec(shape, lambda *_: zeros, pipeline_mode=pl.Buffered(1))


def _mod_kernel(c_ref, w_ref, b_ref, o_ref):
    o_ref[...] = _dot(c_ref[...].astype(BF16), w_ref[...].astype(BF16)) + b_ref[...]


def _modulation(c, w, b):
    nb, d = c.shape
    n = w.shape[1]
    out = pl.pallas_call(
        _mod_kernel,
        grid=(n // d,),
        in_specs=[pl.BlockSpec((nb, d), lambda j: (0, 0)),
                  pl.BlockSpec((d, d), lambda j: (0, j)),
                  pl.BlockSpec((1, d), lambda j: (0, j))],
        out_specs=pl.BlockSpec((nb, d), lambda j: (0, j)),
        out_shape=jax.ShapeDtypeStruct((nb, n), F32),
        compiler_params=_params("parallel"),
        name="modulation",
    )(c, w, b.reshape(1, n))
    return out.reshape(nb, N_MOD, 1, d)


def _prep_kernel(wqn_ref, wuk_ref, wuv_ref, wpb_ref, lq1_ref, lk1_ref, lq2_ref, lk2_ref,
                 wqabs_ref, wqabst_ref, wuvpb_ref, lam_ref, *, lam_init):
    hi = lax.Precision.HIGHEST
    nt = (((1,), (1,)), ((), ()))
    for h in range(MLA_HEADS):
        qa = lax.dot_general(wqn_ref[h], wuk_ref[h], nt, precision=hi, preferred_element_type=F32)
        wqabs_ref[:, h * MLA_KV_RANK:(h + 1) * MLA_KV_RANK] = (qa * MLA_SCALE).astype(BF16)
        qat = lax.dot_general(wuk_ref[h], wqn_ref[h], nt, precision=hi, preferred_element_type=F32)
        wqabst_ref[h * MLA_KV_RANK:(h + 1) * MLA_KV_RANK, :] = (qat * (MLA_SCALE * LOG2E)).astype(BF16)
        vp = jnp.dot(wuv_ref[h], wpb_ref[h], precision=hi, preferred_element_type=F32)
        wuvpb_ref[h * MLA_KV_RANK:(h + 1) * MLA_KV_RANK, :] = vp.astype(BF16)
    s1 = jnp.sum(lq1_ref[...] * lk1_ref[...], axis=-1, keepdims=True)
    s2 = jnp.sum(lq2_ref[...] * lk2_ref[...], axis=-1, keepdims=True)
    lam_ref[...] = jnp.exp(s1) - jnp.exp(s2) + lam_init


def _prep(w_uq, w_uk, w_uv, w_pb, lq1, lk1, lq2, lk2, lam_init):
    d_model = w_pb.shape[1]
    wq = w_uq.reshape(MLA_Q_RANK, MLA_HEADS, MLA_NOPE + MLA_ROPE)
    wqn = jnp.transpose(wq[:, :, :MLA_NOPE], (1, 0, 2))
    wuk = jnp.transpose(w_uk, (1, 0, 2))
    wuv = jnp.transpose(w_uv, (1, 0, 2))
    wpb = w_pb.reshape(MLA_HEADS, MLA_V, d_model)
    row = lambda a: a.reshape(1, -1)
    return pl.pallas_call(
        functools.partial(_prep_kernel, lam_init=lam_init),
        out_shape=(jax.ShapeDtypeStruct((MLA_Q_RANK, MLA_HEADS * MLA_KV_RANK), BF16),
                   jax.ShapeDtypeStruct((MLA_HEADS * MLA_KV_RANK, MLA_Q_RANK), BF16),
                   jax.ShapeDtypeStruct((MLA_HEADS * MLA_KV_RANK, d_model), BF16),
                   jax.ShapeDtypeStruct((1, 1), F32)),
        compiler_params=pltpu.CompilerParams(vmem_limit_bytes=VMEM_LIMIT_BYTES),
        name="fold_weights",
    )(wqn, wuk, wuv, wpb, row(lq1), row(lk1), row(lq2), row(lk2))


def _bucket(n):
    bucket = jnp.full(n.shape, REL_EXACT, jnp.int32)
    for start in _BUCKET_STARTS:
        bucket = bucket + (n >= start).astype(jnp.int32)
    return jnp.where(n < REL_EXACT, n, bucket)


def _bias_rel(bucket, tab_ref, h):
    far = tab_ref[REL_BUCKETS - 1, h]
    val = jnp.zeros(bucket.shape, F32)
    for b in range(REL_BUCKETS - 1):
        val = jnp.where(bucket == b, tab_ref[b, h] - far, val)
    return val


def _bias_kernel(tab_ref, tiles_ref, last_ref, new_ref, *, tq, page, dec_seq, new_pad):
    r = lax.broadcasted_iota(jnp.int32, (tq, tq), 1)
    c = lax.broadcasted_iota(jnp.int32, (tq, tq), 0)
    for t in range(2):
        n = r - c + t * tq
        bk = _bucket(jnp.maximum(n, 0))
        for h in range(DA_HEADS):
            tiles_ref[t, h] = jnp.where(n >= 0, _bias_rel(bk, tab_ref, h) * LOG2E, NEG)
        tiles_ref[t, DA_HEADS] = jnp.where(n >= 0, 0.0, NEG)
    tr = lax.broadcasted_iota(jnp.int32, (dec_seq, page), 0)
    tc = lax.broadcasted_iota(jnp.int32, (dec_seq, page), 1)
    bk = _bucket(tr + page - tc)
    for h in range(DA_HEADS):
        last_ref[h] = _bias_rel(bk, tab_ref, h)
    ur = lax.broadcasted_iota(jnp.int32, (dec_seq, new_pad), 0)
    uc = lax.broadcasted_iota(jnp.int32, (dec_seq, new_pad), 1)
    n = ur - uc
    ok = (n >= 0) & (uc < dec_seq)
    bk = _bucket(jnp.maximum(n, 0))
    for h in range(DA_HEADS):
        new_ref[h] = jnp.where(ok, _bias_rel(bk, tab_ref, h), NEG)
    new_ref[DA_HEADS] = jnp.where(ok, 0.0, NEG)


def _bias_tiles(rel_bias, tq, page, dec_seq, new_pad):
    return pl.pallas_call(
        functools.partial(_bias_kernel, tq=tq, page=page, dec_seq=dec_seq, new_pad=new_pad),
        in_specs=[pl.BlockSpec(memory_space=pltpu.SMEM)],
        out_shape=(jax.ShapeDtypeStruct((2, DA_HEADS + 1, tq, tq), F32),
                   jax.ShapeDtypeStruct((DA_HEADS, dec_seq, page), F32),
                   jax.ShapeDtypeStruct((DA_HEADS + 1, dec_seq, new_pad), F32)),
        compiler_params=pltpu.CompilerParams(vmem_limit_bytes=VMEM_LIMIT_BYTES),
        name="bias_tiles",
    )(rel_bias)


def _token_tiling(b, s):
    if s >= 512 and s % 512 == 0:
        return 1, 512
    if s >= 256 and s % 256 == 0:
        return 1, 256
    if s >= LANES:
        assert s % LANES == 0
        return 1, LANES
    assert s % SUBLANES == 0
    bb = min(b, 256 // s)
    assert b % bb == 0
    return bb, s


def _x_spec(bb, ts, d):
    return pl.BlockSpec((bb, ts, d), lambda i, j: (i, j, 0))


def _mod_spec(bb, d, group):
    return pl.BlockSpec((bb, 3, 1, d), lambda i, j: (i, group, 0, 0))


def _rows_spec(rows, width, ns):
    return pl.BlockSpec((rows, width), lambda i, j: (i * ns + j, 0))


def _modulated(x_ref, mod_ref, g_ref):
    x = x_ref[...]
    h = _rms(x, g_ref[...]) * (1.0 + mod_ref[:, 1]) + mod_ref[:, 0]
    bb, ts, d = x.shape
    return x, h.reshape(bb * ts, d).astype(BF16)


def _ffn_kernel(x_ref, mod_ref, g_ref, wup_ref, wdn_ref, *rest, cuts, final_norm):
    if final_norm:
        gf_ref, o_ref = rest
    else:
        (o_ref,) = rest
    x, hb = _modulated(x_ref, mod_ref, g_ref)
    d_ff = wdn_ref.shape[0]
    acc = None
    for lo, hi in zip(cuts[:-1], cuts[1:]):
        g = _dot(hb, wup_ref[:, lo:hi])
        u = _dot(hb, wup_ref[:, d_ff + lo:d_ff + hi])
        a = (g * jax.nn.sigmoid(g) * u).astype(BF16)
        part = _dot(a, wdn_ref[lo:hi, :])
        acc = part if acc is None else acc + part
    out = x + (MACARON_WEIGHT * mod_ref[:, 2]) * acc.reshape(x.shape)
    if final_norm:
        out = _rms(out, gf_ref[...])
    o_ref[...] = out


def _ffn(x, mod, group, g, w_up, w_down, g_final=None):
    b, s, d = x.shape
    bb, ts = _token_tiling(b, s)
    d_ff = w_down.shape[0]
    mid = -(-(d_ff // 2) // MXU_WIDTH) * MXU_WIDTH
    cuts = (0, mid, d_ff) if 0 < mid < d_ff else (0, d_ff)
    in_specs = [_x_spec(bb, ts, d), _mod_spec(bb, d, group), _resident((1, d)),
                _resident(w_up.shape), _resident(w_down.shape)]
    args = [x, mod, g.reshape(1, d), w_up, w_down]
    if g_final is not None:
        in_specs.append(_resident((1, d)))
        args.append(g_final.reshape(1, d))
    return pl.pallas_call(
        functools.partial(_ffn_kernel, cuts=cuts, final_norm=g_final is not None),
        grid=(b // bb, s // ts),
        in_specs=in_specs,
        out_specs=_x_spec(bb, ts, d),
        out_shape=jax.ShapeDtypeStruct(x.shape, F32),
        compiler_params=_params("parallel", "parallel"),
        name="ffn_final" if g_final is not None else "ffn",
    )(*args)


def _rope(z, cos, sin_lo, sin_hi):
    half = MLA_ROPE // 2
    return z * cos + pltpu.roll(z, LANES - half, 1) * sin_lo + pltpu.roll(z, half, 1) * sin_hi


def _mixer_in_kernel(x_ref, mod_ref, g_ref, w_ref, gq_ref, gkv_ref, wqabs_ref, wqrope_ref,
                     cos_ref, slo_ref, shi_ref,
                     qd_ref, kd_ref, kdb_ref, vd_ref, vaug_ref, qcat_ref, kvcat_ref, ckv_ref, kr_ref):
    bb, ts, d = x_ref.shape
    rows = bb * ts
    _, hb = _modulated(x_ref, mod_ref, g_ref)
    qw = DA_HEADS * DA_V
    kw = DA_KV_HEADS * DA_V
    qd_ref[...] = (_dot(hb, w_ref[:, :qw]) * (DA_HEAD_DIM ** -0.5)).astype(BF16)
    kd = _dot(hb, w_ref[:, qw:qw + kw])
    kd_ref[...] = kd
    kdb_ref[...] = kd.astype(BF16)
    vd = _dot(hb, w_ref[:, qw + kw:qw + 2 * kw])
    vd_ref[...] = vd
    ones = jnp.ones((rows, DA_V), BF16)
    for k in range(DA_KV_HEADS):
        vaug_ref[:, 2 * k * DA_V:(2 * k + 1) * DA_V] = vd[:, k * DA_V:(k + 1) * DA_V].astype(BF16)
        vaug_ref[:, (2 * k + 1) * DA_V:(2 * k + 2) * DA_V] = ones
    base = qw + 2 * kw
    lat = _dot(hb, w_ref[:, base:base + MLA_Q_RANK + MLA_KV_RANK + LANES])
    qn = _rms(lat[:, :MLA_Q_RANK], gq_ref[...]).astype(BF16)
    ckv = _rms(lat[:, MLA_Q_RANK:MLA_Q_RANK + MLA_KV_RANK], gkv_ref[...])
    kr = lat[:, MLA_Q_RANK + MLA_KV_RANK:]

    def table(ref):
        return jnp.broadcast_to(ref[...], (bb, ts, LANES)).reshape(rows, LANES)

    cos, slo, shi = table(cos_ref), table(slo_ref), table(shi_ref)
    qabs = _dot(qn, wqabs_ref[...])
    zq = _dot(qn, wqrope_ref[...])
    for h in range(MLA_HEADS):
        lo = 2 * h * LANES
        qcat_ref[:, lo:lo + LANES] = qabs[:, h * LANES:(h + 1) * LANES].astype(BF16)
        qcat_ref[:, lo + LANES:lo + 2 * LANES] = _rope(zq[:, h * LANES:(h + 1) * LANES],
                                                       cos, slo, shi).astype(BF16)
    krr = _rope(kr, cos, slo, shi)
    ckv_ref[...] = ckv
    kr_ref[...] = krr[:, :MLA_ROPE]
    lane = lax.broadcasted_iota(jnp.int32, (rows, LANES), 1)
    kvcat_ref[:, :LANES] = ckv.astype(BF16)
    kvcat_ref[:, LANES:] = jnp.where(lane == LANES - 1, 1.0, krr).astype(BF16)


def _mixer_in(x, mod, g, w_qkv, g_q, g_kv, w_qabs, w_qrope, tables):
    b, s, d = x.shape
    bb, ts = _token_tiling(b, s)
    ns = s // ts
    rows = bb * ts
    n = b * s
    widths = [(DA_HEADS * DA_V, BF16), (DA_KV_HEADS * DA_V, F32), (DA_KV_HEADS * DA_V, BF16),
              (DA_KV_HEADS * DA_V, F32), (2 * DA_KV_HEADS * DA_V, BF16),
              (2 * MLA_HEADS * LANES, BF16), (2 * LANES, BF16), (MLA_KV_RANK, F32), (MLA_ROPE, F32)]
    tab_spec = pl.BlockSpec((1, ts, LANES), lambda i, j: (0, j, 0))
    return pl.pallas_call(
        _mixer_in_kernel,
        grid=(b // bb, ns),
        in_specs=[_x_spec(bb, ts, d), _mod_spec(bb, d, 1), _resident((1, d)), _resident(w_qkv.shape),
                  _resident((1, MLA_Q_RANK)), _resident((1, MLA_KV_RANK)),
                  _resident(w_qabs.shape), _resident(w_qrope.shape), tab_spec, tab_spec, tab_spec],
        out_specs=[_rows_spec(rows, w, ns) for w, _ in widths],
        out_shape=[jax.ShapeDtypeStruct((n, w), dt) for w, dt in widths],
        compiler_params=_params("parallel", "parallel"),
        name="mixer_in",
    )(x, mod, g.reshape(1, d), w_qkv, g_q.reshape(1, -1), g_kv.reshape(1, -1), w_qabs, w_qrope, *tables)


V_AUG = DA_V + 2 * SUBLANES


def _mixer_in_prompt_kernel(x_ref, mod_ref, g_ref, wqt_ref, wk_ref, wkt_ref, wv_ref, wvt_ref, wlat_ref,
                            gq_ref, gkv_ref, wqabst_ref, wqropet_ref, cos_ref, slo_ref, shi_ref,
                            cost_ref, sint_ref,
                            qdt_ref, kdt_ref, kdb_ref, v4_ref, vta_ref, qcatt_ref, kvcat_ref, ckv_ref,
                            cta_ref, krt_ref, *, tk):
    _, ts, _ = x_ref.shape
    _, hb = _modulated(x_ref, mod_ref, g_ref)
    half = MLA_ROPE // 2
    qdt_ref[0] = (_dot_nt(wqt_ref[...], hb) * (DA_HEAD_DIM ** -0.5 * LOG2E)).astype(BF16)
    kdt_ref[0] = _dot_nt(wkt_ref[...], hb)
    kdb_ref[...] = _dot(hb, wk_ref[...]).astype(BF16)
    vd = _dot(hb, wv_ref[...])
    for k in range(DA_KV_HEADS):
        v4_ref[pl.ds(k, ts, stride=DA_KV_HEADS), :] = vd[:, k * DA_V:(k + 1) * DA_V]
    vt = _dot_nt(wvt_ref[...], hb)
    ones = jnp.ones((V_AUG - DA_V, tk), BF16)
    for c in range(ts // tk):
        cols = slice(c * tk, (c + 1) * tk)
        for k in range(DA_KV_HEADS):
            vta_ref[0, c, k * V_AUG:k * V_AUG + DA_V, :] = vt[k * DA_V:(k + 1) * DA_V, cols].astype(BF16)
            vta_ref[0, c, k * V_AUG + DA_V:(k + 1) * V_AUG, :] = ones
    lat = _dot(hb, wlat_ref[...])
    qn = _rms(lat[:, :MLA_Q_RANK], gq_ref[...]).astype(BF16)
    ckv = _rms(lat[:, MLA_Q_RANK:MLA_Q_RANK + MLA_KV_RANK], gkv_ref[...])
    krr = _rope(lat[:, MLA_Q_RANK + MLA_KV_RANK:], cos_ref[0], slo_ref[0], shi_ref[0])
    ckv_ref[...] = ckv
    kvcat_ref[:, :LANES] = ckv.astype(BF16)
    kvcat_ref[:, LANES:] = krr.astype(BF16)
    krt_ref[0] = krr.T[:MLA_ROPE, :]
    ckvt = ckv.T
    for c in range(ts // tk):
        cols = slice(c * tk, (c + 1) * tk)
        cta_ref[0, c, :MLA_KV_RANK, :] = ckvt[:, cols].astype(BF16)
        cta_ref[0, c, MLA_KV_RANK:, :] = ones
    qabst = _dot_nt(wqabst_ref[...], qn)
    zqt = _dot_nt(wqropet_ref[...], qn)
    cost, sint = cost_ref[...], sint_ref[...]
    zeros = jnp.zeros((2 * LANES - MLA_KV_RANK - MLA_ROPE, ts), BF16)
    for h in range(MLA_HEADS):
        lo = 2 * h * LANES
        x1 = zqt[h * MLA_ROPE:h * MLA_ROPE + half]
        x2 = zqt[h * MLA_ROPE + half:(h + 1) * MLA_ROPE]
        qcatt_ref[0, lo:lo + MLA_KV_RANK, :] = qabst[h * MLA_KV_RANK:(h + 1) * MLA_KV_RANK].astype(BF16)
        r0 = lo + MLA_KV_RANK
        qcatt_ref[0, r0:r0 + half, :] = (x1 * cost - x2 * sint).astype(BF16)
        qcatt_ref[0, r0 + half:r0 + MLA_ROPE, :] = (x1 * sint + x2 * cost).astype(BF16)
        qcatt_ref[0, r0 + MLA_ROPE:lo + 2 * LANES, :] = zeros


def _mixer_in_prompt(x, mod, g, wt, g_q, g_kv, w_qabst, w_qropet, tables, tables_t, tk):
    b, s, d = x.shape
    bb, ts = _token_tiling(b, s)
    assert bb == 1 and ts % tk == 0
    ns = s // ts
    nk = s // tk
    n = b * s
    kw = DA_KV_HEADS * DA_V
    feat = lambda w: pl.BlockSpec((1, w, ts), lambda i, j: (i, 0, j))
    blocked = lambda w: pl.BlockSpec((1, ts // tk, w, tk), lambda i, j: (i, j, 0, 0))
    rows = lambda w, mult=1: pl.BlockSpec((ts * mult, w), lambda i, j: (i * ns + j, 0))
    tab_spec = pl.BlockSpec((1, ts, LANES), lambda i, j: (0, j, 0))
    tabt_spec = pl.BlockSpec((MLA_ROPE // 2, ts), lambda i, j: (0, j))
    out_specs = [feat(DA_HEADS * DA_V), feat(kw), rows(kw), rows(DA_V, DA_KV_HEADS),
                 blocked(DA_KV_HEADS * V_AUG), feat(2 * MLA_HEADS * LANES), rows(2 * LANES),
                 rows(MLA_KV_RANK), blocked(V_AUG), feat(MLA_ROPE)]
    out_shape = [jax.ShapeDtypeStruct((b, DA_HEADS * DA_V, s), BF16),
                 jax.ShapeDtypeStruct((b, kw, s), F32),
                 jax.ShapeDtypeStruct((n, kw), BF16),
                 jax.ShapeDtypeStruct((n * DA_KV_HEADS, DA_V), F32),
                 jax.ShapeDtypeStruct((b, nk, DA_KV_HEADS * V_AUG, tk), BF16),
                 jax.ShapeDtypeStruct((b, 2 * MLA_HEADS * LANES, s), BF16),
                 jax.ShapeDtypeStruct((n, 2 * LANES), BF16),
                 jax.ShapeDtypeStruct((n, MLA_KV_RANK), F32),
                 jax.ShapeDtypeStruct((b, nk, V_AUG, tk), BF16),
                 jax.ShapeDtypeStruct((b, MLA_ROPE, s), F32)]
    wqt, wk, wkt, wv, wvt, wlat = wt
    return pl.pallas_call(
        functools.partial(_mixer_in_prompt_kernel, tk=tk),
        grid=(b, ns),
        in_specs=[_x_spec(bb, ts, d), _mod_spec(bb, d, 1), _resident((1, d)),
                  _resident(wqt.shape), _resident(wk.shape), _resident(wkt.shape), _resident(wv.shape),
                  _resident(wvt.shape), _resident(wlat.shape),
                  _resident((1, MLA_Q_RANK)), _resident((1, MLA_KV_RANK)),
                  _resident(w_qabst.shape), _resident(w_qropet.shape),
                  tab_spec, tab_spec, tab_spec, tabt_spec, tabt_spec],
        out_specs=out_specs,
        out_shape=out_shape,
        compiler_params=_params("parallel", "parallel"),
        name="mixer_in_prompt",
    )(x, mod, g.reshape(1, d), wqt, wk, wkt, wv, wvt, wlat, g_q.reshape(1, -1), g_kv.reshape(1, -1),
      w_qabst, w_qropet, *tables, *tables_t)


def _merge_kernel(x_ref, mod_ref, g_ref, wg_ref, od_ref, olat_ref, wpa_ref, wuvpb_ref, wo_ref, o_ref):
    x, hb = _modulated(x_ref, mod_ref, g_ref)
    d = x.shape[-1]
    ya = _dot(od_ref[...].astype(BF16), wpa_ref[...])
    yb = _dot(olat_ref[...].astype(BF16), wuvpb_ref[...])
    ga = jax.nn.sigmoid(_dot(hb, wg_ref[:, :d]))
    gb = jax.nn.sigmoid(_dot(hb, wg_ref[:, d:]))
    mix = (ga * ya + gb * yb).astype(BF16)
    o_ref[...] = x + mod_ref[:, 2] * _dot(mix, wo_ref[...]).reshape(x.shape)


def _merge(x, mod, g, w_gates, od, olat, w_pa, w_uvpb, w_o):
    b, s, d = x.shape
    bb, ts = _token_tiling(b, s)
    ns = s // ts
    rows = bb * ts
    return pl.pallas_call(
        _merge_kernel,
        grid=(b // bb, ns),
        in_specs=[_x_spec(bb, ts, d), _mod_spec(bb, d, 1), _resident((1, d)), _resident(w_gates.shape),
                  _rows_spec(rows, od.shape[1], ns), _rows_spec(rows, olat.shape[1], ns),
                  _resident(w_pa.shape), _resident(w_uvpb.shape), _resident(w_o.shape)],
        out_specs=_x_spec(bb, ts, d),
        out_shape=jax.ShapeDtypeStruct(x.shape, F32),
        compiler_params=_params("parallel", "parallel"),
        name="merge",
    )(x, mod, g.reshape(1, d), w_gates, od, olat, w_pa, w_uvpb, w_o)


def _diff_out(o0, l0, o1, l1, lam, g_subln, out_scale):
    o = o0 / l0 - lam * (o1 / l1)
    return _rms(o, g_subln) * out_scale


def _half_masks(rows):
    lane = lax.broadcasted_iota(jnp.int32, (rows, DA_V), 1)
    return lane < DA_HEAD_DIM


N_DIFF_CHAINS = 2 * DA_HEADS
N_CHAINS = N_DIFF_CHAINS + MLA_HEADS
PIPE_DELAYS = (2, 3, 4, 6)


def _prompt_attn_kernel(lam_ref, qdt_ref, qcatt_ref, kdb_ref, vta_ref, kvcat_ref, cta_ref, tiles_ref,
                        gs_ref, od_ref, olat_ref, qm_scr, m_scr, acc_scr, *, tq, out_scale):
    i = pl.program_id(1)
    lam = lam_ref[0, 0]
    near_lo = jnp.maximum(i - 1, 0)

    zero = jnp.zeros((DA_HEAD_DIM, tq), BF16)
    for hq in range(DA_HEADS):
        q = qdt_ref[0, hq * DA_V:(hq + 1) * DA_V, :]
        qm_scr[2 * hq] = jnp.concatenate([q[:DA_HEAD_DIM], zero], axis=0)
        qm_scr[2 * hq + 1] = jnp.concatenate([zero, q[DA_HEAD_DIM:]], axis=0)
    m_scr[...] = jnp.full(m_scr.shape, -jnp.inf, F32)
    acc_scr[...] = jnp.zeros(acc_scr.shape, F32)

    def block(j, near):
        rows = pl.ds(pl.multiple_of(j * tq, tq), tq)
        t = i - j

        def operands(c):
            if c < N_DIFF_CHAINS:
                hq = c // 2
                k = hq // DA_GROUP
                return (kdb_ref[0, rows, k * DA_V:(k + 1) * DA_V], qm_scr[c],
                        vta_ref[0, j, k * V_AUG:(k + 1) * V_AUG, :], tiles_ref[t, hq] if near else None)
            h = c - N_DIFF_CHAINS
            return (kvcat_ref[0, rows, :], qcatt_ref[0, 2 * h * LANES:(2 * h + 2) * LANES, :],
                    cta_ref[0, j], tiles_ref[t, DA_HEADS] if near else None)

        def scores(c):
            keys, q, _, bias = operands(c)
            s = _dot(keys, q)
            return s if bias is None else s + bias

        def running_max(c, s):
            m_old = m_scr[c]
            m_new = jnp.maximum(m_old, jnp.max(s, axis=0, keepdims=True))
            m_scr[c] = m_new
            return s, m_new, jnp.exp2(m_old - m_new)

        def probs(s, m_new, alpha):
            return jnp.exp2(s - m_new).astype(BF16), alpha

        def weigh(c, p, alpha):
            return _dot(operands(c)[2], p), alpha

        def accumulate(c, pv, alpha):
            acc_scr[c] = acc_scr[c] * alpha + pv

        stages = [(0, lambda c, _: scores(c)), (PIPE_DELAYS[0], running_max),
                  (PIPE_DELAYS[1], lambda c, v: probs(*v)), (PIPE_DELAYS[2], lambda c, v: weigh(c, *v)),
                  (PIPE_DELAYS[3], lambda c, v: accumulate(c, *v))]
        live = {}
        for n in range(N_CHAINS + PIPE_DELAYS[-1]):
            for d, stage in stages:
                c = n - d
                if 0 <= c < N_CHAINS:
                    live[c] = stage(c, live.get(c))

    def far_body(j, carry):
        block(j, False)
        return carry

    def near_body(j, carry):
        block(j, True)
        return carry

    lax.fori_loop(0, near_lo, far_body, 0)
    lax.fori_loop(near_lo, i + 1, near_body, 0)

    gs = gs_ref[...]
    for hq in range(DA_HEADS):
        a0 = acc_scr[2 * hq]
        a1 = acc_scr[2 * hq + 1]
        o = a0[:DA_V] / a0[DA_V:DA_V + 1] - lam * (a1[:DA_V] / a1[DA_V:DA_V + 1])
        o = o * lax.rsqrt(jnp.mean(o * o, axis=0, keepdims=True) + EPS) * gs * out_scale
        od_ref[0, :, hq * DA_V:(hq + 1) * DA_V] = o.T.astype(BF16)
    for h in range(MLA_HEADS):
        a = acc_scr[N_DIFF_CHAINS + h]
        olat_ref[0, :, h * LANES:(h + 1) * LANES] = (a[:MLA_KV_RANK] / a[MLA_KV_RANK:MLA_KV_RANK + 1]).T.astype(BF16)


def _prompt_attention(lam, qdt, qcatt, kdb, vta, kvcat, cta, tiles, g_subln, b, s, out_scale):
    tq = tiles.shape[-1]
    assert s % tq == 0 and vta.shape[-1] == tq
    nk = s // tq
    v3 = lambda a: a.reshape(b, s, a.shape[-1])
    qspec = lambda w: pl.BlockSpec((1, w, tq), lambda bi, i: (bi, 0, i))
    ospec = lambda w: pl.BlockSpec((1, tq, w), lambda bi, i: (bi, i, 0))
    kvspec = lambda w: pl.BlockSpec((1, s, w), lambda bi, i: (bi, 0, 0))
    blocked = lambda w: pl.BlockSpec((1, nk, w, tq), lambda bi, i: (bi, 0, 0, 0))
    od, olat = pl.pallas_call(
        functools.partial(_prompt_attn_kernel, tq=tq, out_scale=out_scale),
        grid=(b, s // tq),
        in_specs=[pl.BlockSpec(memory_space=pltpu.SMEM),
                  qspec(qdt.shape[1]), qspec(qcatt.shape[1]),
                  kvspec(kdb.shape[-1]), blocked(vta.shape[2]), kvspec(kvcat.shape[-1]),
                  blocked(cta.shape[2]), _resident(tiles.shape), _resident((DA_V, 1))],
        out_specs=[ospec(DA_HEADS * DA_V), ospec(MLA_HEADS * MLA_KV_RANK)],
        out_shape=[jax.ShapeDtypeStruct((b, s, DA_HEADS * DA_V), BF16),
                   jax.ShapeDtypeStruct((b, s, MLA_HEADS * MLA_KV_RANK), BF16)],
        scratch_shapes=[pltpu.VMEM((N_DIFF_CHAINS, DA_V, tq), BF16),
                        pltpu.VMEM((N_CHAINS, 1, tq), F32),
                        pltpu.VMEM((N_CHAINS, V_AUG, tq), F32)],
        compiler_params=_params("parallel", "arbitrary"),
        name="prompt_attention",
    )(lam, qdt, qcatt, v3(kdb), vta, v3(kvcat), cta, tiles, g_subln.reshape(DA_V, 1))
    return od.reshape(b * s, -1), olat.reshape(b * s, -1)


def _sample_attn_kernel(pt_ref, lam_ref, qd_ref, qcat_ref, kn_ref, vn_ref, kvn_ref,
                        blast_ref, bnew_ref, mnew_ref, gs_ref, ck_hbm, cv_hbm, cc_hbm, cr_hbm,
                        od_ref, olat_ref,
                        kbuf, vbuf, cbuf, rbuf, sems, qbd_scr, qm_scr, kb_scr, vb_scr, cb_scr, rb_scr,
                        md_scr, ld_scr, ad_scr, mm_scr, lm_scr, am_scr,
                        *, pages, page, dec_seq, n_groups, layer, out_scale):
    b = pl.program_id(0)
    nseq = pl.num_programs(0)
    t = dec_seq

    def group_copies(seq, grp, slot):
        copies = []
        for p in range(pages):
            pid = pt_ref[seq, grp * pages + p]
            copies.append(pltpu.make_async_copy(ck_hbm.at[layer, pid], kbuf.at[slot, p], sems.at[slot, 0, p]))
            copies.append(pltpu.make_async_copy(cv_hbm.at[layer, pid], vbuf.at[slot, p], sems.at[slot, 1, p]))
            copies.append(pltpu.make_async_copy(cc_hbm.at[layer, pid], cbuf.at[slot, p], sems.at[slot, 2, p]))
            copies.append(pltpu.make_async_copy(cr_hbm.at[layer, pid], rbuf.at[slot, p], sems.at[slot, 3, p]))
        return copies

    def start_group(seq, grp, slot):
        for cp in group_copies(seq, grp, slot):
            cp.start()

    @pl.when(b == 0)
    def _():
        start_group(0, 0, 0)

    qd = qd_ref[0]
    first = _half_masks(t)
    zero = jnp.zeros((t, DA_V), F32)
    blocks = []
    for k in range(DA_KV_HEADS):
        for m in range(2):
            for gg in range(DA_GROUP):
                hq = k * DA_GROUP + gg
                keep = first if m == 0 else jnp.logical_not(first)
                piece = jnp.where(keep, qd[:, hq * DA_V:(hq + 1) * DA_V], 0.0)
                blocks.append(jnp.concatenate(
                    [piece if kk == k else zero for kk in range(DA_KV_HEADS)], axis=1))
    qbd_scr[...] = jnp.concatenate(blocks, axis=0).astype(BF16)
    qc = qcat_ref[0]
    qm_scr[...] = jnp.concatenate(
        [qc[:, 2 * h * LANES:(2 * h + 2) * LANES] for h in range(MLA_HEADS)], axis=0).astype(BF16)
    md_scr[...] = jnp.full(md_scr.shape, -jnp.inf, F32)
    mm_scr[...] = jnp.full(mm_scr.shape, -jnp.inf, F32)
    ld_scr[...] = jnp.zeros(ld_scr.shape, F32)
    lm_scr[...] = jnp.zeros(lm_scr.shape, F32)
    ad_scr[...] = jnp.zeros(ad_scr.shape, F32)
    am_scr[...] = jnp.zeros(am_scr.shape, F32)

    def soft(s, m_ref, l_ref):
        m_old = m_ref[...]
        m_new = jnp.maximum(m_old, jnp.max(s, axis=-1, keepdims=True))
        pr = jnp.exp(s - m_new)
        alpha = jnp.exp(m_old - m_new)
        m_ref[...] = m_new
        l_ref[...] = alpha * l_ref[...] + jnp.sum(pr, axis=-1, keepdims=True)
        return pr.astype(BF16), alpha

    def weigh_diff(pb, alpha, vals):
        rk = 2 * DA_GROUP * t
        for k in range(DA_KV_HEADS):
            rows = slice(k * rk, (k + 1) * rk)
            ad_scr[rows, :] = alpha[rows] * ad_scr[rows, :] + _dot(pb[rows], vals(k))

    def weigh_mla(pb, alpha, vals):
        am_scr[...] = alpha * am_scr[...] + _dot(pb, vals)

    def group_body(g, carry):
        slot = lax.rem(g, 2)
        @pl.when(g + 1 < n_groups)
        def _():
            start_group(b, g + 1, 1 - slot)

        @pl.when(jnp.logical_and(g + 1 == n_groups, b + 1 < nseq))
        def _():
            start_group(b + 1, 0, 1 - slot)

        for cp in group_copies(b, g, slot):
            cp.wait()
        for p in range(pages):
            kb_scr[:, p * page:(p + 1) * page] = kbuf[slot, p].astype(BF16)
        is_last = jnp.where(g == n_groups - 1, 1.0, 0.0).astype(F32)
        s_d = _dot(qbd_scr[...], kb_scr[...]) + is_last * blast_ref[...]
        for p in range(pages):
            cb_scr[p * page:(p + 1) * page, :] = cbuf[slot, p].astype(BF16)
            rb_scr[:, p * page:(p + 1) * page] = rbuf[slot, p].astype(BF16)
        s_m = (_dot_nt(qm_scr[:, :MLA_KV_RANK], cb_scr[...])
               + _dot(qm_scr[:, MLA_KV_RANK:MLA_KV_RANK + MLA_ROPE], rb_scr[...]))
        for p in range(pages):
            for k in range(DA_KV_HEADS):
                vb_scr[k, p * page:(p + 1) * page, :] = (
                    vbuf[slot, p, pl.ds(k, page, stride=DA_KV_HEADS), :].astype(BF16))
        pd, alpha_d = soft(s_d, md_scr, ld_scr)
        pm, alpha_m = soft(s_m, mm_scr, lm_scr)
        weigh_diff(pd, alpha_d, lambda k: vb_scr[k])
        weigh_mla(pm, alpha_m, cb_scr[...])
        return carry

    lax.fori_loop(0, n_groups, group_body, 0)

    pad = bnew_ref.shape[-1] - t
    zpad = lambda a: jnp.concatenate([a, jnp.zeros((pad, a.shape[-1]), a.dtype)], axis=0)
    kn = zpad(kn_ref[0]).astype(BF16)
    vn = zpad(vn_ref[0]).astype(BF16)
    kvn = zpad(kvn_ref[0]).astype(BF16)
    pd, alpha_d = soft(_dot_nt(qbd_scr[...], kn) + bnew_ref[...], md_scr, ld_scr)
    pm, alpha_m = soft(_dot_nt(qm_scr[...], kvn) + mnew_ref[...], mm_scr, lm_scr)
    weigh_diff(pd, alpha_d, lambda k: vn[:, k * DA_V:(k + 1) * DA_V])
    weigh_mla(pm, alpha_m, kvn[:, :MLA_KV_RANK])
    lam = lam_ref[0, 0]
    for k in range(DA_KV_HEADS):
        for gg in range(DA_GROUP):
            r0 = ((k * 2 + 0) * DA_GROUP + gg) * t
            r1 = ((k * 2 + 1) * DA_GROUP + gg) * t
            o = _diff_out(ad_scr[r0:r0 + t, :], ld_scr[r0:r0 + t, :],
                          ad_scr[r1:r1 + t, :], ld_scr[r1:r1 + t, :], lam, gs_ref[...], out_scale)
            hq = k * DA_GROUP + gg
            od_ref[0, :, hq * DA_V:(hq + 1) * DA_V] = o
    for h in range(MLA_HEADS):
        olat_ref[0, :, h * LANES:(h + 1) * LANES] = (
            am_scr[h * t:(h + 1) * t, :] / lm_scr[h * t:(h + 1) * t, :])


def _sample_attention(page_table, lam, qd, qcat, kd, vd, kvcat, blast, bnew, mnew, g_subln,
                      cache_k, cache_v, cache_ckv, cache_kr, layer, out_scale):
    nseq, n_pages = page_table.shape
    page = cache_k.shape[2]
    t = qd.shape[0] // nseq
    pages = min(PAGES_PER_STEP, n_pages)
    n_groups = n_pages // pages
    assert n_pages % pages == 0 and n_groups % 2 == 0
    n_keys = pages * page
    kvw = DA_KV_HEADS * DA_V
    depth, n_pool = cache_k.shape[:2]
    ck = jnp.transpose(cache_k, (0, 1, 3, 4, 5, 2)).reshape(depth, n_pool, kvw, page)
    cv = cache_v.reshape(depth, n_pool, page * DA_KV_HEADS, DA_V)
    cr = jnp.transpose(cache_kr, (0, 1, 3, 2))
    seq3 = lambda a: a.reshape(nseq, t, a.shape[-1]).astype(F32)
    seq_spec = lambda w: pl.BlockSpec((1, t, w), lambda b, pt: (b, 0, 0))
    const = lambda shape: pl.BlockSpec(shape, lambda b, pt: (0,) * len(shape))
    hbm = pl.BlockSpec(memory_space=pl.ANY)
    rows_d = DA_KV_HEADS * 2 * DA_GROUP * t
    rows_m = MLA_HEADS * t
    in_specs = [pl.BlockSpec(memory_space=pltpu.SMEM),
                seq_spec(DA_HEADS * DA_V), seq_spec(2 * MLA_HEADS * LANES), seq_spec(kvw), seq_spec(kvw),
                seq_spec(2 * LANES), const(blast.shape), const(bnew.shape), const(mnew.shape),
                const((1, DA_V)), hbm, hbm, hbm, hbm]
    grid_spec = pltpu.PrefetchScalarGridSpec(
        num_scalar_prefetch=1,
        grid=(nseq,),
        in_specs=in_specs,
        out_specs=[seq_spec(DA_HEADS * DA_V), seq_spec(MLA_HEADS * MLA_KV_RANK)],
        scratch_shapes=[pltpu.VMEM((2, pages, kvw, page), F32),
                        pltpu.VMEM((2, pages, page * DA_KV_HEADS, DA_V), F32),
                        pltpu.VMEM((2, pages, page, MLA_KV_RANK), F32),
                        pltpu.VMEM((2, pages, MLA_ROPE, page), F32),
                        pltpu.SemaphoreType.DMA((2, 4, pages)),
                        pltpu.VMEM((rows_d, kvw), BF16), pltpu.VMEM((rows_m, 2 * LANES), BF16),
                        pltpu.VMEM((kvw, n_keys), BF16), pltpu.VMEM((DA_KV_HEADS, n_keys, DA_V), BF16),
                        pltpu.VMEM((n_keys, MLA_KV_RANK), BF16), pltpu.VMEM((MLA_ROPE, n_keys), BF16),
                        pltpu.VMEM((rows_d, 1), F32), pltpu.VMEM((rows_d, 1), F32),
                        pltpu.VMEM((rows_d, DA_V), F32),
                        pltpu.VMEM((rows_m, 1), F32), pltpu.VMEM((rows_m, 1), F32),
                        pltpu.VMEM((rows_m, MLA_KV_RANK), F32)])
    od, olat = pl.pallas_call(
        functools.partial(_sample_attn_kernel, pages=pages, page=page, dec_seq=t, n_groups=n_groups,
                          layer=layer, out_scale=out_scale),
        grid_spec=grid_spec,
        out_shape=[jax.ShapeDtypeStruct((nseq, t, DA_HEADS * DA_V), F32),
                   jax.ShapeDtypeStruct((nseq, t, MLA_HEADS * MLA_KV_RANK), F32)],
        compiler_params=_params("arbitrary"),
        name="sample_attention",
    )(page_table, lam, seq3(qd), seq3(qcat), seq3(kd), seq3(vd), seq3(kvcat), blast, bnew, mnew,
      g_subln.reshape(1, DA_V), ck, cv, cache_ckv, cr)
    return od.reshape(nseq * t, -1), olat.reshape(nseq * t, -1)


def _pack_w_in(w_in):
    qkv = DA_HEADS * DA_V + 2 * DA_KV_HEADS * DA_V + MLA_Q_RANK + MLA_KV_RANK
    pad = jnp.zeros((w_in.shape[0], LANES - MLA_ROPE), w_in.dtype)
    w_qkv = jnp.concatenate([w_in[:, :qkv + MLA_ROPE], pad], axis=1).astype(BF16)
    w_gates = w_in[:, qkv + MLA_ROPE:].astype(BF16)
    return w_qkv, w_gates


def _pack_w_qrope(w_uq):
    wq = w_uq.reshape(MLA_Q_RANK, MLA_HEADS, MLA_NOPE + MLA_ROPE)[:, :, MLA_NOPE:] * MLA_SCALE
    pad = jnp.zeros((MLA_Q_RANK, MLA_HEADS, LANES - MLA_ROPE), w_uq.dtype)
    return jnp.concatenate([wq, pad], axis=2).reshape(MLA_Q_RANK, MLA_HEADS * LANES).astype(BF16)


def _pack_w_prompt(w_in):
    qw = DA_HEADS * DA_V
    kw = DA_KV_HEADS * DA_V
    wq, wk, wv = w_in[:, :qw], w_in[:, qw:qw + kw], w_in[:, qw + kw:qw + 2 * kw]
    base = qw + 2 * kw
    pad = jnp.zeros((w_in.shape[0], LANES - MLA_ROPE), w_in.dtype)
    wlat = jnp.concatenate([w_in[:, base:base + MLA_Q_RANK + MLA_KV_RANK + MLA_ROPE], pad], axis=1)
    return tuple(a.astype(BF16) for a in (wq.T, wk, wk.T, wv, wv.T, wlat))


def _pack_w_qrope_t(w_uq):
    wq = w_uq.reshape(MLA_Q_RANK, MLA_HEADS, MLA_NOPE + MLA_ROPE)[:, :, MLA_NOPE:] * (MLA_SCALE * LOG2E)
    return jnp.transpose(wq, (1, 2, 0)).reshape(MLA_HEADS * MLA_ROPE, MLA_Q_RANK).astype(BF16)


def _rope_tables_t(pos):
    inv = ROPE_BASE ** (-jnp.arange(0, MLA_ROPE, 2, dtype=F32) / MLA_ROPE)
    ang = inv[:, None] * pos.astype(F32)[None, :]
    return jnp.cos(ang), jnp.sin(ang)


def _rope_tables(pos):
    half = MLA_ROPE // 2
    inv = ROPE_BASE ** (-jnp.arange(0, MLA_ROPE, 2, dtype=F32) / MLA_ROPE)
    ang = pos.astype(F32)[:, None] * inv[None, :]
    cos, sin = jnp.cos(ang), jnp.sin(ang)
    z = jnp.zeros((pos.shape[0], LANES - MLA_ROPE), F32)
    zh = jnp.zeros((pos.shape[0], half), F32)
    t_cos = jnp.concatenate([cos, cos, z], axis=1)
    t_lo = jnp.concatenate([-sin, zh, z], axis=1)
    t_hi = jnp.concatenate([zh, sin, z], axis=1)
    return tuple(a[None] for a in (t_cos, t_lo, t_hi))


def _sample_bias_rows(last, new, t, pages, page):
    heads = jnp.array([k * DA_GROUP + g for k in range(DA_KV_HEADS) for _ in range(2)
                       for g in range(DA_GROUP)], jnp.int32)
    blast = last[heads].reshape(-1, page)
    blast = jnp.concatenate([jnp.zeros((blast.shape[0], (pages - 1) * page), F32), blast], axis=1)
    bnew = new[heads].reshape(-1, new.shape[-1])
    mnew = jnp.tile(new[DA_HEADS], (MLA_HEADS, 1))
    return blast, bnew, mnew


def kernel(x_prompt, x_sample, c_prompt, c_sample, cache_diff_k, cache_diff_v, cache_mla_ckv, cache_mla_krope, page_table, rel_bias, w_ada, b_ada, g_ffn1, w_ffn1_up, w_ffn1_down, g_mix, w_in, g_q, w_uq, g_kv, w_uk, w_uv, lambda_q1, lambda_k1, lambda_q2, lambda_k2, g_subln, w_pa, w_pb, w_o, g_ffn2, w_ffn2_up, w_ffn2_down, g_final):
    depth = w_ada.shape[0]
    bp, sp, d = x_prompt.shape
    bs, ss, _ = x_sample.shape
    page = cache_diff_k.shape[2]
    past_len = page_table.shape[1] * page
    pages = min(PAGES_PER_STEP, page_table.shape[1])
    new_pad = 2 * SUBLANES
    tq = min(Q_TILE, sp)

    tiles, b_last, b_new = _bias_tiles(rel_bias, tq, page, ss, new_pad)
    blast, bnew, mnew = _sample_bias_rows(b_last, b_new, ss, pages, page)
    tab_p = _rope_tables(jnp.arange(sp, dtype=jnp.int32))
    tabt_p = _rope_tables_t(jnp.arange(sp, dtype=jnp.int32))
    tab_s = _rope_tables(past_len + jnp.arange(ss, dtype=jnp.int32))
    c_all = jnp.concatenate([c_prompt, c_sample], axis=0)

    xp, xs = x_prompt, x_sample
    st_p, st_s = [], []
    for l in range(depth):
        lam_init = 0.8 - 0.6 * math.exp(-0.3 * l)
        out_scale = 1.0 - lam_init
        mod = _modulation(c_all, w_ada[l], b_ada[l])
        mod_p, mod_s = mod[:bp], mod[bp:]
        w_qabs, w_qabst, w_uvpb, lam = _prep(w_uq[l], w_uk[l], w_uv[l], w_pb[l], lambda_q1[l],
                                             lambda_k1[l], lambda_q2[l], lambda_k2[l], lam_init)
        w_qkv, w_gates = _pack_w_in(w_in[l])
        w_prompt = _pack_w_prompt(w_in[l])
        w_qrope = _pack_w_qrope(w_uq[l])
        w_qropet = _pack_w_qrope_t(w_uq[l])
        w1u, w1d = w_ffn1_up[l].astype(BF16), w_ffn1_down[l].astype(BF16)
        w2u, w2d = w_ffn2_up[l].astype(BF16), w_ffn2_down[l].astype(BF16)
        wpa, wo = w_pa[l].astype(BF16), w_o[l].astype(BF16)
        g_last = g_final if l == depth - 1 else None

        xp = _ffn(xp, mod_p, 0, g_ffn1[l], w1u, w1d)
        qdt, kdt, kdb, v4, vta, qcatt, kvcat, ckv, cta, krt = _mixer_in_prompt(
            xp, mod_p, g_mix[l], w_prompt, g_q[l], g_kv[l], w_qabst, w_qropet, tab_p, tabt_p, tq)
        od, olat = _prompt_attention(lam, qdt, qcatt, kdb, vta, kvcat, cta, tiles, g_subln[l], bp, sp,
                                     out_scale)
        xp = _merge(xp, mod_p, g_mix[l], w_gates, od, olat, wpa, w_uvpb, wo)
        xp = _ffn(xp, mod_p, 2, g_ffn2[l], w2u, w2d, g_last)
        st_p.append((jnp.transpose(kdt.reshape(bp, DA_KV_HEADS, 2, DA_HEAD_DIM, sp), (0, 4, 1, 2, 3)),
                     v4.reshape(bp, sp, DA_KV_HEADS, DA_V),
                     ckv.reshape(bp, sp, MLA_KV_RANK),
                     jnp.transpose(krt, (0, 2, 1))))

        xs = _ffn(xs, mod_s, 0, g_ffn1[l], w1u, w1d)
        qd, kd, _, vd, _, qcat, kvc, ckv_s, kr_s = _mixer_in(
            xs, mod_s, g_mix[l], w_qkv, g_q[l], g_kv[l], w_qabs, w_qrope, tab_s)
        od, olat = _sample_attention(page_table, lam, qd, qcat, kd, vd, kvc, blast, bnew, mnew,
                                     g_subln[l], cache_diff_k, cache_diff_v, cache_mla_ckv,
                                     cache_mla_krope, l, out_scale)
        xs = _merge(xs, mod_s, g_mix[l], w_gates, od, olat, wpa, w_uvpb, wo)
        xs = _ffn(xs, mod_s, 2, g_ffn2[l], w2u, w2d, g_last)
        st_s.append((kd.reshape(bs, ss, DA_KV_HEADS, 2, DA_HEAD_DIM), vd.reshape(bs, ss, DA_KV_HEADS, DA_V),
                     ckv_s.reshape(bs, ss, MLA_KV_RANK), kr_s.reshape(bs, ss, MLA_ROPE)))

    stack = lambda parts: parts[0][None] if len(parts) == 1 else jnp.stack(parts)
    return (xp, xs) + tuple(stack([s[i] for s in st]) for st in (st_p, st_s) for i in range(4))
```

```python
import functools
import math

import jax
import jax.numpy as jnp
from jax import lax
from jax.experimental import pallas as pl
from jax.experimental.pallas import tpu as pltpu

F32 = jnp.float32
BF16 = jnp.bfloat16

DA_HEADS = 8
DA_KV_HEADS = 4
DA_HEAD_DIM = 64
DA_GROUP = DA_HEADS // DA_KV_HEADS
DA_V = 2 * DA_HEAD_DIM
MLA_HEADS = 8
MLA_Q_RANK = 256
MLA_KV_RANK = 128
MLA_NOPE = 64
MLA_ROPE = 32
MLA_V = 64
MLA_SCALE = 1.0 / math.sqrt(MLA_NOPE + MLA_ROPE)
LOG2E = math.log2(math.e)
ROPE_BASE = 10000.0
REL_BUCKETS = 32
REL_EXACT = REL_BUCKETS // 2
REL_MAX_DIST = 128
MACARON_WEIGHT = 0.5
N_MOD = 9
EPS = 1e-6
NEG = -1e30

LANES = 128
SUBLANES = 8
MXU_WIDTH = 256
VMEM_LIMIT_BYTES = 58 * 1024 * 1024

_BUCKET_STARTS = tuple(
    int(math.ceil(REL_EXACT * (REL_MAX_DIST / REL_EXACT) ** (k / (REL_BUCKETS - REL_EXACT))))
    for k in range(1, REL_BUCKETS - REL_EXACT))

Q_TILE = 256
PAGES_PER_STEP = 8


def _rms(x, g):
    return x * lax.rsqrt(jnp.mean(x * x, axis=-1, keepdims=True) + EPS) * g


def _dot(a, b):
    return jnp.dot(a, b, preferred_element_type=F32)


def _dot_nt(a, b):
    return lax.dot_general(a, b, (((1,), (1,)), ((), ())), preferred_element_type=F32)


def _params(*semantics):
    return pltpu.CompilerParams(dimension_semantics=semantics, vmem_limit_bytes=VMEM_LIMIT_BYTES)


def _resident(shape):
    zeros = (0,) * len(shape)
    return pl.BlockSpec(shape, lambda *_: zeros, pipeline_mode=pl.Buffered(1))


def _mod_kernel(c_ref, w_ref, b_ref, o_ref):
    o_ref[...] = _dot(c_ref[...].astype(BF16), w_ref[...].astype(BF16)) + b_ref[...]


def _modulation(c, w, b):
    nb, d = c.shape
    n = w.shape[1]
    out = pl.pallas_call(
        _mod_kernel,
        grid=(n // d,),
        in_specs=[pl.BlockSpec((nb, d), lambda j: (0, 0)),
                  pl.BlockSpec((d, d), lambda j: (0, j)),
                  pl.BlockSpec((1, d), lambda j: (0, j))],
        out_specs=pl.BlockSpec((nb, d), lambda j: (0, j)),
        out_shape=jax.ShapeDtypeStruct((nb, n), F32),
        compiler_params=_params("parallel"),
        name="modulation",
    )(c, w, b.reshape(1, n))
    return out.reshape(nb, N_MOD, 1, d)


def _prep_kernel(wqn_ref, wuk_ref, wuv_ref, wpb_ref, lq1_ref, lk1_ref, lq2_ref, lk2_ref,
                 wqabs_ref, wqabst_ref, wuvpb_ref, lam_ref, *, lam_init):
    hi = lax.Precision.HIGHEST
    nt = (((1,), (1,)), ((), ()))
    for h in range(MLA_HEADS):
        qa = lax.dot_general(wqn_ref[h], wuk_ref[h], nt, precision=hi, preferred_element_type=F32)
        wqabs_ref[:, h * MLA_KV_RANK:(h + 1) * MLA_KV_RANK] = (qa * MLA_SCALE).astype(BF16)
        qat = lax.dot_general(wuk_ref[h], wqn_ref[h], nt, precision=hi, preferred_element_type=F32)
        wqabst_ref[h * MLA_KV_RANK:(h + 1) * MLA_KV_RANK, :] = (qat * (MLA_SCALE * LOG2E)).astype(BF16)
        vp = jnp.dot(wuv_ref[h], wpb_ref[h], precision=hi, preferred_element_type=F32)
        wuvpb_ref[h * MLA_KV_RANK:(h + 1) * MLA_KV_RANK, :] = vp.astype(BF16)
    s1 = jnp.sum(lq1_ref[...] * lk1_ref[...], axis=-1, keepdims=True)
    s2 = jnp.sum(lq2_ref[...] * lk2_ref[...], axis=-1, keepdims=True)
    lam_ref[...] = jnp.exp(s1) - jnp.exp(s2) + lam_init


def _prep(w_uq, w_uk, w_uv, w_pb, lq1, lk1, lq2, lk2, lam_init):
    d_model = w_pb.shape[1]
    wq = w_uq.reshape(MLA_Q_RANK, MLA_HEADS, MLA_NOPE + MLA_ROPE)
    wqn = jnp.transpose(wq[:, :, :MLA_NOPE], (1, 0, 2))
    wuk = jnp.transpose(w_uk, (1, 0, 2))
    wuv = jnp.transpose(w_uv, (1, 0, 2))
    wpb = w_pb.reshape(MLA_HEADS, MLA_V, d_model)
    row = lambda a: a.reshape(1, -1)
    return pl.pallas_call(
        functools.partial(_prep_kernel, lam_init=lam_init),
        out_shape=(jax.ShapeDtypeStruct((MLA_Q_RANK, MLA_HEADS * MLA_KV_RANK), BF16),
                   jax.ShapeDtypeStruct((MLA_HEADS * MLA_KV_RANK, MLA_Q_RANK), BF16),
                   jax.ShapeDtypeStruct((MLA_HEADS * MLA_KV_RANK, d_model), BF16),
                   jax.ShapeDtypeStruct((1, 1), F32)),
        compiler_params=pltpu.CompilerParams(vmem_limit_bytes=VMEM_LIMIT_BYTES),
        name="fold_weights",
    )(wqn, wuk, wuv, wpb, row(lq1), row(lk1), row(lq2), row(lk2))


def _bucket(n):
    bucket = jnp.full(n.shape, REL_EXACT, jnp.int32)
    for start in _BUCKET_STARTS:
        bucket = bucket + (n >= start).astype(jnp.int32)
    return jnp.where(n < REL_EXACT, n, bucket)


def _bias_rel(bucket, tab_ref, h):
    far = tab_ref[REL_BUCKETS - 1, h]
    val = jnp.zeros(bucket.shape, F32)
    for b in range(REL_BUCKETS - 1):
        val = jnp.where(bucket == b, tab_ref[b, h] - far, val)
    return val


def _bias_kernel(tab_ref, tiles_ref, last_ref, new_ref, *, tq, page, dec_seq, new_pad):
    r = lax.broadcasted_iota(jnp.int32, (tq, tq), 1)
    c = lax.broadcasted_iota(jnp.int32, (tq, tq), 0)
    for t in range(2):
        n = r - c + t * tq
        bk = _bucket(jnp.maximum(n, 0))
        for h in range(DA_HEADS):
            tiles_ref[t, h] = jnp.where(n >= 0, _bias_rel(bk, tab_ref, h) * LOG2E, NEG)
        tiles_ref[t, DA_HEADS] = jnp.where(n >= 0, 0.0, NEG)
    tr = lax.broadcasted_iota(jnp.int32, (dec_seq, page), 0)
    tc = lax.broadcasted_iota(jnp.int32, (dec_seq, page), 1)
    bk = _bucket(tr + page - tc)
    for h in range(DA_HEADS):
        last_ref[h] = _bias_rel(bk, tab_ref, h)
    ur = lax.broadcasted_iota(jnp.int32, (dec_seq, new_pad), 0)
    uc = lax.broadcasted_iota(jnp.int32, (dec_seq, new_pad), 1)
    n = ur - uc
    ok = (n >= 0) & (uc < dec_seq)
    bk = _bucket(jnp.maximum(n, 0))
    for h in range(DA_HEADS):
        new_ref[h] = jnp.where(ok, _bias_rel(bk, tab_ref, h), NEG)
    new_ref[DA_HEADS] = jnp.where(ok, 0.0, NEG)


def _bias_tiles(rel_bias, tq, page, dec_seq, new_pad):
    return pl.pallas_call(
        functools.partial(_bias_kernel, tq=tq, page=page, dec_seq=dec_seq, new_pad=new_pad),
        in_specs=[pl.BlockSpec(memory_space=pltpu.SMEM)],
        out_shape=(jax.ShapeDtypeStruct((2, DA_HEADS + 1, tq, tq), F32),
                   jax.ShapeDtypeStruct((DA_HEADS, dec_seq, page), F32),
                   jax.ShapeDtypeStruct((DA_HEADS + 1, dec_seq, new_pad), F32)),
        compiler_params=pltpu.CompilerParams(vmem_limit_bytes=VMEM_LIMIT_BYTES),
        name="bias_tiles",
    )(rel_bias)


def _token_tiling(b, s):
    if s >= 512 and s % 512 == 0:
        return 1, 512
    if s >= 256 and s % 256 == 0:
        return 1, 256
    if s >= LANES:
        assert s % LANES == 0
        return 1, LANES
    assert s % SUBLANES == 0
    bb = min(b, 256 // s)
    assert b % bb == 0
    return bb, s


def _x_spec(bb, ts, d):
    return pl.BlockSpec((bb, ts, d), lambda i, j: (i, j, 0))


def _mod_spec(bb, d, group):
    return pl.BlockSpec((bb, 3, 1, d), lambda i, j: (i, group, 0, 0))


def _rows_spec(rows, width, ns):
    return pl.BlockSpec((rows, width), lambda i, j: (i * ns + j, 0))


def _modulated(x_ref, mod_ref, g_ref):
    x = x_ref[...]
    h = _rms(x, g_ref[...]) * (1.0 + mod_ref[:, 1]) + mod_ref[:, 0]
    bb, ts, d = x.shape
    return x, h.reshape(bb * ts, d).astype(BF16)


def _ffn_kernel(x_ref, mod_ref, g_ref, wup_ref, wdn_ref, *rest, cuts, final_norm):
    if final_norm:
        gf_ref, o_ref = rest
    else:
        (o_ref,) = rest
    x, hb = _modulated(x_ref, mod_ref, g_ref)
    d_ff = wdn_ref.shape[0]
    acc = None
    for lo, hi in zip(cuts[:-1], cuts[1:]):
        g = _dot(hb, wup_ref[:, lo:hi])
        u = _dot(hb, wup_ref[:, d_ff + lo:d_ff + hi])
        a = (g * jax.nn.sigmoid(g) * u).astype(BF16)
        part = _dot(a, wdn_ref[lo:hi, :])
        acc = part if acc is None else acc + part
    out = x + (MACARON_WEIGHT * mod_ref[:, 2]) * acc.reshape(x.shape)
    if final_norm:
        out = _rms(out, gf_ref[...])
    o_ref[...] = out


def _ffn(x, mod, group, g, w_up, w_down, g_final=None):
    b, s, d = x.shape
    bb, ts = _token_tiling(b, s)
    d_ff = w_down.shape[0]
    mid = -(-(d_ff // 2) // MXU_WIDTH) * MXU_WIDTH
    cuts = (0, mid, d_ff) if 0 < mid < d_ff else (0, d_ff)
    in_specs = [_x_spec(bb, ts, d), _mod_spec(bb, d, group), _resident((1, d)),
                _resident(w_up.shape), _resident(w_down.shape)]
    args = [x, mod, g.reshape(1, d), w_up, w_down]
    if g_final is not None:
        in_specs.append(_resident((1, d)))
        args.append(g_final.reshape(1, d))
    return pl.pallas_call(
        functools.partial(_ffn_kernel, cuts=cuts, final_norm=g_final is not None),
        grid=(b // bb, s // ts),
        in_specs=in_specs,
        out_specs=_x_spec(bb, ts, d),
        out_shape=jax.ShapeDtypeStruct(x.shape, F32),
        compiler_params=_params("parallel", "parallel"),
        name="ffn_final" if g_final is not None else "ffn",
    )(*args)


def _rope(z, cos, sin_lo, sin_hi):
    half = MLA_ROPE // 2
    return z * cos + pltpu.roll(z, LANES - half, 1) * sin_lo + pltpu.roll(z, half, 1) * sin_hi


def _mixer_in_kernel(x_ref, mod_ref, g_ref, w_ref, gq_ref, gkv_ref, wqabs_ref, wqrope_ref,
                     cos_ref, slo_ref, shi_ref,
                     qd_ref, kd_ref, kdb_ref, vd_ref, vaug_ref, qcat_ref, kvcat_ref, ckv_ref, kr_ref):
    bb, ts, d = x_ref.shape
    rows = bb * ts
    _, hb = _modulated(x_ref, mod_ref, g_ref)
    qw = DA_HEADS * DA_V
    kw = DA_KV_HEADS * DA_V
    qd_ref[...] = (_dot(hb, w_ref[:, :qw]) * (DA_HEAD_DIM ** -0.5)).astype(BF16)
    kd = _dot(hb, w_ref[:, qw:qw + kw])
    kd_ref[...] = kd
    kdb_ref[...] = kd.astype(BF16)
    vd = _dot(hb, w_ref[:, qw + kw:qw + 2 * kw])
    vd_ref[...] = vd
    ones = jnp.ones((rows, DA_V), BF16)
    for k in range(DA_KV_HEADS):
        vaug_ref[:, 2 * k * DA_V:(2 * k + 1) * DA_V] = vd[:, k * DA_V:(k + 1) * DA_V].astype(BF16)
        vaug_ref[:, (2 * k + 1) * DA_V:(2 * k + 2) * DA_V] = ones
    base = qw + 2 * kw
    lat = _dot(hb, w_ref[:, base:base + MLA_Q_RANK + MLA_KV_RANK + LANES])
    qn = _rms(lat[:, :MLA_Q_RANK], gq_ref[...]).astype(BF16)
    ckv = _rms(lat[:, MLA_Q_RANK:MLA_Q_RANK + MLA_KV_RANK], gkv_ref[...])
    kr = lat[:, MLA_Q_RANK + MLA_KV_RANK:]

    def table(ref):
        return jnp.broadcast_to(ref[...], (bb, ts, LANES)).reshape(rows, LANES)

    cos, slo, shi = table(cos_ref), table(slo_ref), table(shi_ref)
    qabs = _dot(qn, wqabs_ref[...])
    zq = _dot(qn, wqrope_ref[...])
    for h in range(MLA_HEADS):
        lo = 2 * h * LANES
        qcat_ref[:, lo:lo + LANES] = qabs[:, h * LANES:(h + 1) * LANES].astype(BF16)
        qcat_ref[:, lo + LANES:lo + 2 * LANES] = _rope(zq[:, h * LANES:(h + 1) * LANES],
                                                       cos, slo, shi).astype(BF16)
    krr = _rope(kr, cos, slo, shi)
    ckv_ref[...] = ckv
    kr_ref[...] = krr[:, :MLA_ROPE]
    lane = lax.broadcasted_iota(jnp.int32, (rows, LANES), 1)
    kvcat_ref[:, :LANES] = ckv.astype(BF16)
    kvcat_ref[:, LANES:] = jnp.where(lane == LANES - 1, 1.0, krr).astype(BF16)


def _mixer_in(x, mod, g, w_qkv, g_q, g_kv, w_qabs, w_qrope, tables):
    b, s, d = x.shape
    bb, ts = _token_tiling(b, s)
    ns = s // ts
    rows = bb * ts
    n = b * s
    widths = [(DA_HEADS * DA_V, BF16), (DA_KV_HEADS * DA_V, F32), (DA_KV_HEADS * DA_V, BF16),
              (DA_KV_HEADS * DA_V, F32), (2 * DA_KV_HEADS * DA_V, BF16),
              (2 * MLA_HEADS * LANES, BF16), (2 * LANES, BF16), (MLA_KV_RANK, F32), (MLA_ROPE, F32)]
    tab_spec = pl.BlockSpec((1, ts, LANES), lambda i, j: (0, j, 0))
    return pl.pallas_call(
        _mixer_in_kernel,
        grid=(b // bb, ns),
        in_specs=[_x_spec(bb, ts, d), _mod_spec(bb, d, 1), _resident((1, d)), _resident(w_qkv.shape),
                  _resident((1, MLA_Q_RANK)), _resident((1, MLA_KV_RANK)),
                  _resident(w_qabs.shape), _resident(w_qrope.shape), tab_spec, tab_spec, tab_spec],
        out_specs=[_rows_spec(rows, w, ns) for w, _ in widths],
        out_shape=[jax.ShapeDtypeStruct((n, w), dt) for w, dt in widths],
        compiler_params=_params("parallel", "parallel"),
        name="mixer_in",
    )(x, mod, g.reshape(1, d), w_qkv, g_q.reshape(1, -1), g_kv.reshape(1, -1), w_qabs, w_qrope, *tables)


V_AUG = DA_V + 2 * SUBLANES


def _mixer_in_prompt_kernel(x_ref, mod_ref, g_ref, wqt_ref, wk_ref, wkt_ref, wv_ref, wvt_ref, wlat_ref,
                            gq_ref, gkv_ref, wqabst_ref, wqropet_ref, cos_ref, slo_ref, shi_ref,
                            cost_ref, sint_ref,
                            qdt_ref, kdt_ref, kdb_ref, v4_ref, vta_ref, qcatt_ref, kvcat_ref, ckv_ref,
                            cta_ref, krt_ref, *, tk):
    _, ts, _ = x_ref.shape
    _, hb = _modulated(x_ref, mod_ref, g_ref)
    half = MLA_ROPE // 2
    qdt_ref[0] = (_dot_nt(wqt_ref[...], hb) * (DA_HEAD_DIM ** -0.5 * LOG2E)).astype(BF16)
    kdt_ref[0] = _dot_nt(wkt_ref[...], hb)
    kdb_ref[...] = _dot(hb, wk_ref[...]).astype(BF16)
    vd = _dot(hb, wv_ref[...])
    for k in range(DA_KV_HEADS):
        v4_ref[pl.ds(k, ts, stride=DA_KV_HEADS), :] = vd[:, k * DA_V:(k + 1) * DA_V]
    vt = _dot_nt(wvt_ref[...], hb)
    ones = jnp.ones((V_AUG - DA_V, tk), BF16)
    for c in range(ts // tk):
        cols = slice(c * tk, (c + 1) * tk)
        for k in range(DA_KV_HEADS):
            vta_ref[0, c, k * V_AUG:k * V_AUG + DA_V, :] = vt[k * DA_V:(k + 1) * DA_V, cols].astype(BF16)
            vta_ref[0, c, k * V_AUG + DA_V:(k + 1) * V_AUG, :] = ones
    lat = _dot(hb, wlat_ref[...])
    qn = _rms(lat[:, :MLA_Q_RANK], gq_ref[...]).astype(BF16)
    ckv = _rms(lat[:, MLA_Q_RANK:MLA_Q_RANK + MLA_KV_RANK], gkv_ref[...])
    krr = _rope(lat[:, MLA_Q_RANK + MLA_KV_RANK:], cos_ref[0], slo_ref[0], shi_ref[0])
    ckv_ref[...] = ckv
    kvcat_ref[:, :LANES] = ckv.astype(BF16)
    kvcat_ref[:, LANES:] = krr.astype(BF16)
    krt_ref[0] = krr.T[:MLA_ROPE, :]
    ckvt = ckv.T
    for c in range(ts // tk):
        cols = slice(c * tk, (c + 1) * tk)
        cta_ref[0, c, :MLA_KV_RANK, :] = ckvt[:, cols].astype(BF16)
        cta_ref[0, c, MLA_KV_RANK:, :] = ones
    qabst = _dot_nt(wqabst_ref[...], qn)
    zqt = _dot_nt(wqropet_ref[...], qn)
    cost, sint = cost_ref[...], sint_ref[...]
    zeros = jnp.zeros((2 * LANES - MLA_KV_RANK - MLA_ROPE, ts), BF16)
    for h in range(MLA_HEADS):
        lo = 2 * h * LANES
        x1 = zqt[h * MLA_ROPE:h * MLA_ROPE + half]
        x2 = zqt[h * MLA_ROPE + half:(h + 1) * MLA_ROPE]
        qcatt_ref[0, lo:lo + MLA_KV_RANK, :] = qabst[h * MLA_KV_RANK:(h + 1) * MLA_KV_RANK].astype(BF16)
        r0 = lo + MLA_KV_RANK
        qcatt_ref[0, r0:r0 + half, :] = (x1 * cost - x2 * sint).astype(BF16)
        qcatt_ref[0, r0 + half:r0 + MLA_ROPE, :] = (x1 * sint + x2 * cost).astype(BF16)
        qcatt_ref[0, r0 + MLA_ROPE:lo + 2 * LANES, :] = zeros


def _mixer_in_prompt(x, mod, g, wt, g_q, g_kv, w_qabst, w_qropet, tables, tables_t, tk):
    b, s, d = x.shape
    bb, ts = _token_tiling(b, s)
    assert bb == 1 and ts % tk == 0
    ns = s // ts
    nk = s // tk
    n = b * s
    kw = DA_KV_HEADS * DA_V
    feat = lambda w: pl.BlockSpec((1, w, ts), lambda i, j: (i, 0, j))
    blocked = lambda w: pl.BlockSpec((1, ts // tk, w, tk), lambda i, j: (i, j, 0, 0))
    rows = lambda w, mult=1: pl.BlockSpec((ts * mult, w), lambda i, j: (i * ns + j, 0))
    tab_spec = pl.BlockSpec((1, ts, LANES), lambda i, j: (0, j, 0))
    tabt_spec = pl.BlockSpec((MLA_ROPE // 2, ts), lambda i, j: (0, j))
    out_specs = [feat(DA_HEADS * DA_V), feat(kw), rows(kw), rows(DA_V, DA_KV_HEADS),
                 blocked(DA_KV_HEADS * V_AUG), feat(2 * MLA_HEADS * LANES), rows(2 * LANES),
                 rows(MLA_KV_RANK), blocked(V_AUG), feat(MLA_ROPE)]
    out_shape = [jax.ShapeDtypeStruct((b, DA_HEADS * DA_V, s), BF16),
                 jax.ShapeDtypeStruct((b, kw, s), F32),
                 jax.ShapeDtypeStruct((n, kw), BF16),
                 jax.ShapeDtypeStruct((n * DA_KV_HEADS, DA_V), F32),
                 jax.ShapeDtypeStruct((b, nk, DA_KV_HEADS * V_AUG, tk), BF16),
                 jax.ShapeDtypeStruct((b, 2 * MLA_HEADS * LANES, s), BF16),
                 jax.ShapeDtypeStruct((n, 2 * LANES), BF16),
                 jax.ShapeDtypeStruct((n, MLA_KV_RANK), F32),
                 jax.ShapeDtypeStruct((b, nk, V_AUG, tk), BF16),
                 jax.ShapeDtypeStruct((b, MLA_ROPE, s), F32)]
    wqt, wk, wkt, wv, wvt, wlat = wt
    return pl.pallas_call(
        functools.partial(_mixer_in_prompt_kernel, tk=tk),
        grid=(b, ns),
        in_specs=[_x_spec(bb, ts, d), _mod_spec(bb, d, 1), _resident((1, d)),
                  _resident(wqt.shape), _resident(wk.shape), _resident(wkt.shape), _resident(wv.shape),
                  _resident(wvt.shape), _resident(wlat.shape),
                  _resident((1, MLA_Q_RANK)), _resident((1, MLA_KV_RANK)),
                  _resident(w_qabst.shape), _resident(w_qropet.shape),
                  tab_spec, tab_spec, tab_spec, tabt_spec, tabt_spec],
        out_specs=out_specs,
        out_shape=out_shape,
        compiler_params=_params("parallel", "parallel"),
        name="mixer_in_prompt",
    )(x, mod, g.reshape(1, d), wqt, wk, wkt, wv, wvt, wlat, g_q.reshape(1, -1), g_kv.reshape(1, -1),
      w_qabst, w_qropet, *tables, *tables_t)


def _merge_kernel(x_ref, mod_ref, g_ref, wg_ref, od_ref, olat_ref, wpa_ref, wuvpb_ref, wo_ref, o_ref):
    x, hb = _modulated(x_ref, mod_ref, g_ref)
    d = x.shape[-1]
    ya = _dot(od_ref[...].astype(BF16), wpa_ref[...])
    yb = _dot(olat_ref[...].astype(BF16), wuvpb_ref[...])
    ga = jax.nn.sigmoid(_dot(hb, wg_ref[:, :d]))
    gb = jax.nn.sigmoid(_dot(hb, wg_ref[:, d:]))
    mix = (ga * ya + gb * yb).astype(BF16)
    o_ref[...] = x + mod_ref[:, 2] * _dot(mix, wo_ref[...]).reshape(x.shape)


def _merge(x, mod, g, w_gates, od, olat, w_pa, w_uvpb, w_o):
    b, s, d = x.shape
    bb, ts = _token_tiling(b, s)
    ns = s // ts
    rows = bb * ts
    return pl.pallas_call(
        _merge_kernel,
        grid=(b // bb, ns),
        in_specs=[_x_spec(bb, ts, d), _mod_spec(bb, d, 1), _resident((1, d)), _resident(w_gates.shape),
                  _rows_spec(rows, od.shape[1], ns), _rows_spec(rows, olat.shape[1], ns),
                  _resident(w_pa.shape), _resident(w_uvpb.shape), _resident(w_o.shape)],
        out_specs=_x_spec(bb, ts, d),
        out_shape=jax.ShapeDtypeStruct(x.shape, F32),
        compiler_params=_params("parallel", "parallel"),
        name="merge",
    )(x, mod, g.reshape(1, d), w_gates, od, olat, w_pa, w_uvpb, w_o)


def _diff_out(o0, l0, o1, l1, lam, g_subln, out_scale):
    o = o0 / l0 - lam * (o1 / l1)
    return _rms(o, g_subln) * out_scale


def _half_masks(rows):
    lane = lax.broadcasted_iota(jnp.int32, (rows, DA_V), 1)
    return lane < DA_HEAD_DIM


N_DIFF_CHAINS = 2 * DA_HEADS
N_CHAINS = N_DIFF_CHAINS + MLA_HEADS
PIPE_DELAYS = (2, 3, 4, 6)


def _prompt_attn_kernel(lam_ref, qdt_ref, qcatt_ref, kdb_ref, vta_ref, kvcat_ref, cta_ref, tiles_ref,
                        gs_ref, od_ref, olat_ref, qm_scr, m_scr, acc_scr, *, tq, out_scale):
    i = pl.program_id(1)
    lam = lam_ref[0, 0]
    near_lo = jnp.maximum(i - 1, 0)

    zero = jnp.zeros((DA_HEAD_DIM, tq), BF16)
    for hq in range(DA_HEADS):
        q = qdt_ref[0, hq * DA_V:(hq + 1) * DA_V, :]
        qm_scr[2 * hq] = jnp.concatenate([q[:DA_HEAD_DIM], zero], axis=0)
        qm_scr[2 * hq + 1] = jnp.concatenate([zero, q[DA_HEAD_DIM:]], axis=0)
    m_scr[...] = jnp.full(m_scr.shape, -jnp.inf, F32)
    acc_scr[...] = jnp.zeros(acc_scr.shape, F32)

    def block(j, near):
        rows = pl.ds(pl.multiple_of(j * tq, tq), tq)
        t = i - j

        def operands(c):
            if c < N_DIFF_CHAINS:
                hq = c // 2
                k = hq // DA_GROUP
                return (kdb_ref[0, rows, k * DA_V:(k + 1) * DA_V], qm_scr[c],
                        vta_ref[0, j, k * V_AUG:(k + 1) * V_AUG, :], tiles_ref[t, hq] if near else None)
            h = c - N_DIFF_CHAINS
            return (kvcat_ref[0, rows, :], qcatt_ref[0, 2 * h * LANES:(2 * h + 2) * LANES, :],
                    cta_ref[0, j], tiles_ref[t, DA_HEADS] if near else None)

        def scores(c):
            keys, q, _, bias = operands(c)
            s = _dot(keys, q)
            return s if bias is None else s + bias

        def running_max(c, s):
            m_old = m_scr[c]
            m_new = jnp.maximum(m_old, jnp.max(s, axis=0, keepdims=True))
            m_scr[c] = m_new
            return s, m_new, jnp.exp2(m_old - m_new)

        def probs(s, m_new, alpha):
            return jnp.exp2(s - m_new).astype(BF16), alpha

        def weigh(c, p, alpha):
            return _dot(operands(c)[2], p), alpha

        def accumulate(c, pv, alpha):
            acc_scr[c] = acc_scr[c] * alpha + pv

        stages = [(0, lambda c, _: scores(c)), (PIPE_DELAYS[0], running_max),
                  (PIPE_DELAYS[1], lambda c, v: probs(*v)), (PIPE_DELAYS[2], lambda c, v: weigh(c, *v)),
                  (PIPE_DELAYS[3], lambda c, v: accumulate(c, *v))]
        live = {}
        for n in range(N_CHAINS + PIPE_DELAYS[-1]):
            for d, stage in stages:
                c = n - d
                if 0 <= c < N_CHAINS:
                    live[c] = stage(c, live.get(c))

    def far_body(j, carry):
        block(j, False)
        return carry

    def near_body(j, carry):
        block(j, True)
        return carry

    lax.fori_loop(0, near_lo, far_body, 0)
    lax.fori_loop(near_lo, i + 1, near_body, 0)

    gs = gs_ref[...]
    for hq in range(DA_HEADS):
        a0 = acc_scr[2 * hq]
        a1 = acc_scr[2 * hq + 1]
        o = a0[:DA_V] / a0[DA_V:DA_V + 1] - lam * (a1[:DA_V] / a1[DA_V:DA_V + 1])
        o = o * lax.rsqrt(jnp.mean(o * o, axis=0, keepdims=True) + EPS) * gs * out_scale
        od_ref[0, :, hq * DA_V:(hq + 1) * DA_V] = o.T.astype(BF16)
    for h in range(MLA_HEADS):
        a = acc_scr[N_DIFF_CHAINS + h]
        olat_ref[0, :, h * LANES:(h + 1) * LANES] = (a[:MLA_KV_RANK] / a[MLA_KV_RANK:MLA_KV_RANK + 1]).T.astype(BF16)


def _prompt_attention(lam, qdt, qcatt, kdb, vta, kvcat, cta, tiles, g_subln, b, s, out_scale):
    tq = tiles.shape[-1]
    assert s % tq == 0 and vta.shape[-1] == tq
    nk = s // tq
    v3 = lambda a: a.reshape(b, s, a.shape[-1])
    qspec = lambda w: pl.BlockSpec((1, w, tq), lambda bi, i: (bi, 0, i))
    ospec = lambda w: pl.BlockSpec((1, tq, w), lambda bi, i: (bi, i, 0))
    kvspec = lambda w: pl.BlockSpec((1, s, w), lambda bi, i: (bi, 0, 0))
    blocked = lambda w: pl.BlockSpec((1, nk, w, tq), lambda bi, i: (bi, 0, 0, 0))
    od, olat = pl.pallas_call(
        functools.partial(_prompt_attn_kernel, tq=tq, out_scale=out_scale),
        grid=(b, s // tq),
        in_specs=[pl.BlockSpec(memory_space=pltpu.SMEM),
                  qspec(qdt.shape[1]), qspec(qcatt.shape[1]),
                  kvspec(kdb.shape[-1]), blocked(vta.shape[2]), kvspec(kvcat.shape[-1]),
                  blocked(cta.shape[2]), _resident(tiles.shape), _resident((DA_V, 1))],
        out_specs=[ospec(DA_HEADS * DA_V), ospec(MLA_HEADS * MLA_KV_RANK)],
        out_shape=[jax.ShapeDtypeStruct((b, s, DA_HEADS * DA_V), BF16),
                   jax.ShapeDtypeStruct((b, s, MLA_HEADS * MLA_KV_RANK), BF16)],
        scratch_shapes=[pltpu.VMEM((N_DIFF_CHAINS, DA_V, tq), BF16),
                        pltpu.VMEM((N_CHAINS, 1, tq), F32),
                        pltpu.VMEM((N_CHAINS, V_AUG, tq), F32)],
        compiler_params=_params("parallel", "arbitrary"),
        name="prompt_attention",
    )(lam, qdt, qcatt, v3(kdb), vta, v3(kvcat), cta, tiles, g_subln.reshape(DA_V, 1))
    return od.reshape(b * s, -1), olat.reshape(b * s, -1)


def _sample_attn_kernel(pt_ref, lam_ref, qd_ref, qcat_ref, kn_ref, vn_ref, kvn_ref,
                        blast_ref, bnew_ref, mnew_ref, gs_ref, ck_hbm, cv_hbm, cc_hbm, cr_hbm,
                        od_ref, olat_ref,
                        kbuf, vbuf, cbuf, rbuf, sems, qbd_scr, qm_scr, kb_scr, vb_scr, cb_scr, rb_scr,
                        md_scr, ld_scr, ad_scr, mm_scr, lm_scr, am_scr,
                        *, pages, page, dec_seq, n_groups, layer, out_scale):
    b = pl.program_id(0)
    nseq = pl.num_programs(0)
    t = dec_seq

    def group_copies(seq, grp, slot):
        copies = []
        for p in range(pages):
            pid = pt_ref[seq, grp * pages + p]
            copies.append(pltpu.make_async_copy(ck_hbm.at[layer, pid], kbuf.at[slot, p], sems.at[slot, 0, p]))
            copies.append(pltpu.make_async_copy(cv_hbm.at[layer, pid], vbuf.at[slot, p], sems.at[slot, 1, p]))
            copies.append(pltpu.make_async_copy(cc_hbm.at[layer, pid], cbuf.at[slot, p], sems.at[slot, 2, p]))
            copies.append(pltpu.make_async_copy(cr_hbm.at[layer, pid], rbuf.at[slot, p], sems.at[slot, 3, p]))
        return copies

    def start_group(seq, grp, slot):
        for n, cp in enumerate(group_copies(seq, grp, slot)):
            cp.start(priority=n % 2)

    @pl.when(b == 0)
    def _():
        start_group(0, 0, 0)

    qd = qd_ref[0]
    first = _half_masks(t)
    zero = jnp.zeros((t, DA_V), F32)
    blocks = []
    for k in range(DA_KV_HEADS):
        for m in range(2):
            for gg in range(DA_GROUP):
                hq = k * DA_GROUP + gg
                keep = first if m == 0 else jnp.logical_not(first)
                piece = jnp.where(keep, qd[:, hq * DA_V:(hq + 1) * DA_V], 0.0)
                blocks.append(jnp.concatenate(
                    [piece if kk == k else zero for kk in range(DA_KV_HEADS)], axis=1))
    qbd_scr[...] = jnp.concatenate(blocks, axis=0).astype(BF16)
    qc = qcat_ref[0]
    qm_scr[...] = jnp.concatenate(
        [qc[:, 2 * h * LANES:(2 * h + 2) * LANES] for h in range(MLA_HEADS)], axis=0).astype(BF16)
    md_scr[...] = jnp.full(md_scr.shape, -jnp.inf, F32)
    mm_scr[...] = jnp.full(mm_scr.shape, -jnp.inf, F32)
    ld_scr[...] = jnp.zeros(ld_scr.shape, F32)
    lm_scr[...] = jnp.zeros(lm_scr.shape, F32)
    ad_scr[...] = jnp.zeros(ad_scr.shape, F32)
    am_scr[...] = jnp.zeros(am_scr.shape, F32)

    def soft(s, m_ref, l_ref):
        m_old = m_ref[...]
        m_new = jnp.maximum(m_old, jnp.max(s, axis=-1, keepdims=True))
        pr = jnp.exp(s - m_new)
        alpha = jnp.exp(m_old - m_new)
        m_ref[...] = m_new
        l_ref[...] = alpha * l_ref[...] + jnp.sum(pr, axis=-1, keepdims=True)
        return pr.astype(BF16), alpha

    def weigh_diff(pb, alpha, vals):
        rk = 2 * DA_GROUP * t
        for k in range(DA_KV_HEADS):
            rows = slice(k * rk, (k + 1) * rk)
            ad_scr[rows, :] = alpha[rows] * ad_scr[rows, :] + _dot(pb[rows], vals(k))

    def weigh_mla(pb, alpha, vals):
        am_scr[...] = alpha * am_scr[...] + _dot(pb, vals)

    def group_body(g, carry):
        slot = lax.rem(g, 2)
        @pl.when(g + 1 < n_groups)
        def _():
            start_group(b, g + 1, 1 - slot)

        @pl.when(jnp.logical_and(g + 1 == n_groups, b + 1 < nseq))
        def _():
            start_group(b + 1, 0, 1 - slot)

        for cp in group_copies(b, g, slot):
            cp.wait()
        for p in range(pages):
            kb_scr[:, p * page:(p + 1) * page] = kbuf[slot, p].astype(BF16)
        is_last = jnp.where(g == n_groups - 1, 1.0, 0.0).astype(F32)
        s_d = _dot(qbd_scr[...], kb_scr[...]) + is_last * blast_ref[...]
        for p in range(pages):
            cb_scr[p * page:(p + 1) * page, :] = cbuf[slot, p].astype(BF16)
            rb_scr[:, p * page:(p + 1) * page] = rbuf[slot, p].astype(BF16)
        s_m = (_dot_nt(qm_scr[:, :MLA_KV_RANK], cb_scr[...])
               + _dot(qm_scr[:, MLA_KV_RANK:MLA_KV_RANK + MLA_ROPE], rb_scr[...]))
        for p in range(pages):
            for k in range(DA_KV_HEADS):
                vb_scr[k, p * page:(p + 1) * page, :] = (
                    vbuf[slot, p, pl.ds(k, page, stride=DA_KV_HEADS), :].astype(BF16))
        pd, alpha_d = soft(s_d, md_scr, ld_scr)
        pm, alpha_m = soft(s_m, mm_scr, lm_scr)
        weigh_diff(pd, alpha_d, lambda k: vb_scr[k])
        weigh_mla(pm, alpha_m, cb_scr[...])
        return carry

    lax.fori_loop(0, n_groups, group_body, 0)

    pad = bnew_ref.shape[-1] - t
    zpad = lambda a: jnp.concatenate([a, jnp.zeros((pad, a.shape[-1]), a.dtype)], axis=0)
    kn = zpad(kn_ref[0]).astype(BF16)
    vn = zpad(vn_ref[0]).astype(BF16)
    kvn = zpad(kvn_ref[0]).astype(BF16)
    pd, alpha_d = soft(_dot_nt(qbd_scr[...], kn) + bnew_ref[...], md_scr, ld_scr)
    pm, alpha_m = soft(_dot_nt(qm_scr[...], kvn) + mnew_ref[...], mm_scr, lm_scr)
    weigh_diff(pd, alpha_d, lambda k: vn[:, k * DA_V:(k + 1) * DA_V])
    weigh_mla(pm, alpha_m, kvn[:, :MLA_KV_RANK])
    lam = lam_ref[0, 0]
    for k in range(DA_KV_HEADS):
        for gg in range(DA_GROUP):
            r0 = ((k * 2 + 0) * DA_GROUP + gg) * t
            r1 = ((k * 2 + 1) * DA_GROUP + gg) * t
            o = _diff_out(ad_scr[r0:r0 + t, :], ld_scr[r0:r0 + t, :],
                          ad_scr[r1:r1 + t, :], ld_scr[r1:r1 + t, :], lam, gs_ref[...], out_scale)
            hq = k * DA_GROUP + gg
            od_ref[0, :, hq * DA_V:(hq + 1) * DA_V] = o
    for h in range(MLA_HEADS):
        olat_ref[0, :, h * LANES:(h + 1) * LANES] = (
            am_scr[h * t:(h + 1) * t, :] / lm_scr[h * t:(h + 1) * t, :])


def _sample_attention(page_table, lam, qd, qcat, kd, vd, kvcat, blast, bnew, mnew, g_subln,
                      cache_k, cache_v, cache_ckv, cache_kr, layer, out_scale):
    nseq, n_pages = page_table.shape
    page = cache_k.shape[2]
    t = qd.shape[0] // nseq
    pages = min(PAGES_PER_STEP, n_pages)
    n_groups = n_pages // pages
    assert n_pages % pages == 0 and n_groups % 2 == 0
    n_keys = pages * page
    kvw = DA_KV_HEADS * DA_V
    depth, n_pool = cache_k.shape[:2]
    ck = jnp.transpose(cache_k, (0, 1, 3, 4, 5, 2)).reshape(depth, n_pool, kvw, page)
    cv = cache_v.reshape(depth, n_pool, page * DA_KV_HEADS, DA_V)
    cr = jnp.transpose(cache_kr, (0, 1, 3, 2))
    seq3 = lambda a: a.reshape(nseq, t, a.shape[-1]).astype(F32)
    seq_spec = lambda w: pl.BlockSpec((1, t, w), lambda b, pt: (b, 0, 0))
    const = lambda shape: pl.BlockSpec(shape, lambda b, pt: (0,) * len(shape))
    hbm = pl.BlockSpec(memory_space=pl.ANY)
    rows_d = DA_KV_HEADS * 2 * DA_GROUP * t
    rows_m = MLA_HEADS * t
    in_specs = [pl.BlockSpec(memory_space=pltpu.SMEM),
                seq_spec(DA_HEADS * DA_V), seq_spec(2 * MLA_HEADS * LANES), seq_spec(kvw), seq_spec(kvw),
                seq_spec(2 * LANES), const(blast.shape), const(bnew.shape), const(mnew.shape),
                const((1, DA_V)), hbm, hbm, hbm, hbm]
    grid_spec = pltpu.PrefetchScalarGridSpec(
        num_scalar_prefetch=1,
        grid=(nseq,),
        in_specs=in_specs,
        out_specs=[seq_spec(DA_HEADS * DA_V), seq_spec(MLA_HEADS * MLA_KV_RANK)],
        scratch_shapes=[pltpu.VMEM((2, pages, kvw, page), F32),
                        pltpu.VMEM((2, pages, page * DA_KV_HEADS, DA_V), F32),
                        pltpu.VMEM((2, pages, page, MLA_KV_RANK), F32),
                        pltpu.VMEM((2, pages, MLA_ROPE, page), F32),
                        pltpu.SemaphoreType.DMA((2, 4, pages)),
                        pltpu.VMEM((rows_d, kvw), BF16), pltpu.VMEM((rows_m, 2 * LANES), BF16),
                        pltpu.VMEM((kvw, n_keys), BF16), pltpu.VMEM((DA_KV_HEADS, n_keys, DA_V), BF16),
                        pltpu.VMEM((n_keys, MLA_KV_RANK), BF16), pltpu.VMEM((MLA_ROPE, n_keys), BF16),
                        pltpu.VMEM((rows_d, 1), F32), pltpu.VMEM((rows_d, 1), F32),
                        pltpu.VMEM((rows_d, DA_V), F32),
                        pltpu.VMEM((rows_m, 1), F32), pltpu.VMEM((rows_m, 1), F32),
                        pltpu.VMEM((rows_m, MLA_KV_RANK), F32)])
    od, olat = pl.pallas_call(
        functools.partial(_sample_attn_kernel, pages=pages, page=page, dec_seq=t, n_groups=n_groups,
                          layer=layer, out_scale=out_scale),
        grid_spec=grid_spec,
        out_shape=[jax.ShapeDtypeStruct((nseq, t, DA_HEADS * DA_V), F32),
                   jax.ShapeDtypeStruct((nseq, t, MLA_HEADS * MLA_KV_RANK), F32)],
        compiler_params=_params("arbitrary"),
        name="sample_attention",
    )(page_table, lam, seq3(qd), seq3(qcat), seq3(kd), seq3(vd), seq3(kvcat), blast, bnew, mnew,
      g_subln.reshape(1, DA_V), ck, cv, cache_ckv, cr)
    return od.reshape(nseq * t, -1), olat.reshape(nseq * t, -1)


def _pack_w_in(w_in):
    qkv = DA_HEADS * DA_V + 2 * DA_KV_HEADS * DA_V + MLA_Q_RANK + MLA_KV_RANK
    pad = jnp.zeros((w_in.shape[0], LANES - MLA_ROPE), w_in.dtype)
    w_qkv = jnp.concatenate([w_in[:, :qkv + MLA_ROPE], pad], axis=1).astype(BF16)
    w_gates = w_in[:, qkv + MLA_ROPE:].astype(BF16)
    return w_qkv, w_gates


def _pack_w_qrope(w_uq):
    wq = w_uq.reshape(MLA_Q_RANK, MLA_HEADS, MLA_NOPE + MLA_ROPE)[:, :, MLA_NOPE:] * MLA_SCALE
    pad = jnp.zeros((MLA_Q_RANK, MLA_HEADS, LANES - MLA_ROPE), w_uq.dtype)
    return jnp.concatenate([wq, pad], axis=2).reshape(MLA_Q_RANK, MLA_HEADS * LANES).astype(BF16)


def _pack_w_prompt(w_in):
    qw = DA_HEADS * DA_V
    kw = DA_KV_HEADS * DA_V
    wq, wk, wv = w_in[:, :qw], w_in[:, qw:qw + kw], w_in[:, qw + kw:qw + 2 * kw]
    base = qw + 2 * kw
    pad = jnp.zeros((w_in.shape[0], LANES - MLA_ROPE), w_in.dtype)
    wlat = jnp.concatenate([w_in[:, base:base + MLA_Q_RANK + MLA_KV_RANK + MLA_ROPE], pad], axis=1)
    return tuple(a.astype(BF16) for a in (wq.T, wk, wk.T, wv, wv.T, wlat))


def _pack_w_qrope_t(w_uq):
    wq = w_uq.reshape(MLA_Q_RANK, MLA_HEADS, MLA_NOPE + MLA_ROPE)[:, :, MLA_NOPE:] * (MLA_SCALE * LOG2E)
    return jnp.transpose(wq, (1, 2, 0)).reshape(MLA_HEADS * MLA_ROPE, MLA_Q_RANK).astype(BF16)


def _rope_tables_t(pos):
    inv = ROPE_BASE ** (-jnp.arange(0, MLA_ROPE, 2, dtype=F32) / MLA_ROPE)
    ang = inv[:, None] * pos.astype(F32)[None, :]
    return jnp.cos(ang), jnp.sin(ang)


def _rope_tables(pos):
    half = MLA_ROPE // 2
    inv = ROPE_BASE ** (-jnp.arange(0, MLA_ROPE, 2, dtype=F32) / MLA_ROPE)
    ang = pos.astype(F32)[:, None] * inv[None, :]
    cos, sin = jnp.cos(ang), jnp.sin(ang)
    z = jnp.zeros((pos.shape[0], LANES - MLA_ROPE), F32)
    zh = jnp.zeros((pos.shape[0], half), F32)
    t_cos = jnp.concatenate([cos, cos, z], axis=1)
    t_lo = jnp.concatenate([-sin, zh, z], axis=1)
    t_hi = jnp.concatenate([zh, sin, z], axis=1)
    return tuple(a[None] for a in (t_cos, t_lo, t_hi))


def _sample_bias_rows(last, new, t, pages, page):
    heads = jnp.array([k * DA_GROUP + g for k in range(DA_KV_HEADS) for _ in range(2)
                       for g in range(DA_GROUP)], jnp.int32)
    blast = last[heads].reshape(-1, page)
    blast = jnp.concatenate([jnp.zeros((blast.shape[0], (pages - 1) * page), F32), blast], axis=1)
    bnew = new[heads].reshape(-1, new.shape[-1])
    mnew = jnp.tile(new[DA_HEADS], (MLA_HEADS, 1))
    return blast, bnew, mnew


def kernel(x_prompt, x_sample, c_prompt, c_sample, cache_diff_k, cache_diff_v, cache_mla_ckv, cache_mla_krope, page_table, rel_bias, w_ada, b_ada, g_ffn1, w_ffn1_up, w_ffn1_down, g_mix, w_in, g_q, w_uq, g_kv, w_uk, w_uv, lambda_q1, lambda_k1, lambda_q2, lambda_k2, g_subln, w_pa, w_pb, w_o, g_ffn2, w_ffn2_up, w_ffn2_down, g_final):
    depth = w_ada.shape[0]
    bp, sp, d = x_prompt.shape
    bs, ss, _ = x_sample.shape
    page = cache_diff_k.shape[2]
    past_len = page_table.shape[1] * page
    pages = min(PAGES_PER_STEP, page_table.shape[1])
    new_pad = 2 * SUBLANES
    tq = min(Q_TILE, sp)

    tiles, b_last, b_new = _bias_tiles(rel_bias, tq, page, ss, new_pad)
    blast, bnew, mnew = _sample_bias_rows(b_last, b_new, ss, pages, page)
    tab_p = _rope_tables(jnp.arange(sp, dtype=jnp.int32))
    tabt_p = _rope_tables_t(jnp.arange(sp, dtype=jnp.int32))
    tab_s = _rope_tables(past_len + jnp.arange(ss, dtype=jnp.int32))
    c_all = jnp.concatenate([c_prompt, c_sample], axis=0)

    xp, xs = x_prompt, x_sample
    st_p, st_s = [], []
    for l in range(depth):
        lam_init = 0.8 - 0.6 * math.exp(-0.3 * l)
        out_scale = 1.0 - lam_init
        mod = _modulation(c_all, w_ada[l], b_ada[l])
        mod_p, mod_s = mod[:bp], mod[bp:]
        w_qabs, w_qabst, w_uvpb, lam = _prep(w_uq[l], w_uk[l], w_uv[l], w_pb[l], lambda_q1[l],
                                             lambda_k1[l], lambda_q2[l], lambda_k2[l], lam_init)
        w_qkv, w_gates = _pack_w_in(w_in[l])
        w_prompt = _pack_w_prompt(w_in[l])
        w_qrope = _pack_w_qrope(w_uq[l])
        w_qropet = _pack_w_qrope_t(w_uq[l])
        w1u, w1d = w_ffn1_up[l].astype(BF16), w_ffn1_down[l].astype(BF16)
        w2u, w2d = w_ffn2_up[l].astype(BF16), w_ffn2_down[l].astype(BF16)
        wpa, wo = w_pa[l].astype(BF16), w_o[l].astype(BF16)
        g_last = g_final if l == depth - 1 else None

        xp = _ffn(xp, mod_p, 0, g_ffn1[l], w1u, w1d)
        qdt, kdt, kdb, v4, vta, qcatt, kvcat, ckv, cta, krt = _mixer_in_prompt(
            xp, mod_p, g_mix[l], w_prompt, g_q[l], g_kv[l], w_qabst, w_qropet, tab_p, tabt_p, tq)
        od, olat = _prompt_attention(lam, qdt, qcatt, kdb, vta, kvcat, cta, tiles, g_subln[l], bp, sp,
                                     out_scale)
        xp = _merge(xp, mod_p, g_mix[l], w_gates, od, olat, wpa, w_uvpb, wo)
        xp = _ffn(xp, mod_p, 2, g_ffn2[l], w2u, w2d, g_last)
        st_p.append((jnp.transpose(kdt.reshape(bp, DA_KV_HEADS, 2, DA_HEAD_DIM, sp), (0, 4, 1, 2, 3)),
                     v4.reshape(bp, sp, DA_KV_HEADS, DA_V),
                     ckv.reshape(bp, sp, MLA_KV_RANK),
                     jnp.transpose(krt, (0, 2, 1))))

        xs = _ffn(xs, mod_s, 0, g_ffn1[l], w1u, w1d)
        qd, kd, _, vd, _, qcat, kvc, ckv_s, kr_s = _mixer_in(
            xs, mod_s, g_mix[l], w_qkv, g_q[l], g_kv[l], w_qabs, w_qrope, tab_s)
        od, olat = _sample_attention(page_table, lam, qd, qcat, kd, vd, kvc, blast, bnew, mnew,
                                     g_subln[l], cache_diff_k, cache_diff_v, cache_mla_ckv,
                                     cache_mla_krope, l, out_scale)
        xs = _merge(xs, mod_s, g_mix[l], w_gates, od, olat, wpa, w_uvpb, wo)
        xs = _ffn(xs, mod_s, 2, g_ffn2[l], w2u, w2d, g_last)
        st_s.append((kd.reshape(bs, ss, DA_KV_HEADS, 2, DA_HEAD_DIM), vd.reshape(bs, ss, DA_KV_HEADS, DA_V),
                     ckv_s.reshape(bs, ss, MLA_KV_RANK), kr_s.reshape(bs, ss, MLA_ROPE)))

    stack = lambda parts: parts[0][None] if len(parts) == 1 else jnp.stack(parts)
    return (xp, xs) + tuple(stack([s[i] for s in st]) for st in (st_p, st_s) for i in range(4))
```
